```python
import math
import jax
import jax.numpy as jnp
from jax import lax
import numpy as np

D_MODEL = 2048
BATCH = 1
SEQ = 8192
DEPTH = 1
DEC_BATCH = 32
DEC_SEQ = 1
PAST_LEN = 8192
PAGE_SIZE = 128

HEAD_DIM = 128
NSA_HEADS = D_MODEL // 256
NSA_KV_HEADS = 2
NSA_GROUP = NSA_HEADS // NSA_KV_HEADS
CMP_BLOCK = 32
CMP_STRIDE = 16
SEL_BLOCK = 64
N_SEL = 16
WINDOW = 512
PHI_HIDDEN = 2 * HEAD_DIM
MOBA_HEADS = D_MODEL // 256
MOBA_BLOCK = 256
MOBA_TOPK = 3
ROPE_THETA = 10000.0
NORM_EPS = 1e-6
QBLK = 64
ROW_ALIGN = MOBA_BLOCK
W_A = NSA_HEADS * HEAD_DIM
W_B = MOBA_HEADS * HEAD_DIM
KV_A = NSA_KV_HEADS * HEAD_DIM
SPLIT_SIZES = (W_A, 6 * KV_A, 3 * NSA_HEADS, W_A, 3 * W_B, W_B, 2 * D_MODEL)
SPLIT_POINTS = tuple(int(s) for s in np.cumsum(SPLIT_SIZES)[:-1])
D_IN = int(sum(SPLIT_SIZES))

kernel_name = 'nsa_moba_gated_hybrid_step'


def round_up(n, m):
    return -(-n // m) * m


def rmsnorm(x, g):
    xf = x.astype(jnp.float32)
    ms = jnp.mean(xf * xf, axis=-1, keepdims=True)
    return (xf * lax.rsqrt(ms + NORM_EPS) * g).astype(x.dtype)


def rope(x, pos):
    half = HEAD_DIM // 2
    inv = ROPE_THETA ** (-(jnp.arange(half, dtype=jnp.float32) / half))
    ang = pos.astype(jnp.float32)[:, None] * inv[None, :]
    cos = jnp.cos(ang)[None, :, None, :]
    sin = jnp.sin(ang)[None, :, None, :]
    xf = x.astype(jnp.float32)
    x1, x2 = xf[..., :half], xf[..., half:]
    return jnp.concatenate([x1 * cos - x2 * sin, x2 * cos + x1 * sin], axis=-1).astype(x.dtype)


def masked_softmax(logits, mask):
    logits = jnp.where(mask, logits.astype(jnp.float32), -jnp.inf)
    m = jnp.max(logits, axis=-1, keepdims=True)
    m = jnp.where(jnp.isfinite(m), m, 0.0)
    p = jnp.exp(logits - m)
    return p / jnp.maximum(jnp.sum(p, axis=-1, keepdims=True), 1e-30)


def pad_rows(a, lp):
    return jnp.pad(a, [(0, 0), (0, lp - a.shape[1])] + [(0, 0)] * (a.ndim - 2))


def gather_pages(cache, layer, page_table):
    rows = cache[layer, page_table]
    b, n_pages, page = rows.shape[:3]
    return rows.reshape((b, n_pages * page) + rows.shape[3:])


def project(x, c, pos, w_ada, b_ada, norm_g, w_in):
    b, t, _ = x.shape
    mod = jax.nn.silu(c) @ w_ada + b_ada
    shift, scale, gate = jnp.split(mod, 3, axis=-1)
    h = rmsnorm(x, norm_g) * (1.0 + scale[:, None, :]) + shift[:, None, :]
    q_a, kv_a, g_a, z_a, qkv_b, z_b, g_m = jnp.split(h @ w_in, SPLIT_POINTS, axis=-1)
    q_a = rope(q_a.reshape(b, t, NSA_HEADS, HEAD_DIM), pos)
    kv_a = kv_a.reshape(b, t, 6, NSA_KV_HEADS, HEAD_DIM)
    k_cmp = rope(kv_a[:, :, 0], pos)
    k_slc = rope(kv_a[:, :, 2], pos)
    k_win = rope(kv_a[:, :, 4], pos)
    nsa_rows = jnp.stack([k_cmp, kv_a[:, :, 1], k_slc, kv_a[:, :, 3]], axis=2)
    win_rows = jnp.stack([k_win, kv_a[:, :, 5]], axis=2)
    qkv_b = qkv_b.reshape(b, t, 3, MOBA_HEADS, HEAD_DIM)
    q_b = rope(qkv_b[:, :, 0], pos)
    moba_rows = jnp.stack([rope(qkv_b[:, :, 1], pos), qkv_b[:, :, 2]], axis=2)
    g_a = jax.nn.sigmoid(g_a).reshape(b, t, NSA_HEADS, 3)
    g_m = jax.nn.sigmoid(g_m).reshape(b, t, 2, D_MODEL)
    proj = (gate, q_a, g_a, z_a, q_b, z_b, g_m)
    return proj, nsa_rows, moba_rows, win_rows


def compress_rows(rows, n_valid, pe, w1, w2):
    b, lp = rows.shape[:2]
    nc = (n_valid - CMP_BLOCK) // CMP_STRIDE + 1
    sub = rows.reshape(b, lp // CMP_STRIDE, CMP_STRIDE, NSA_KV_HEADS, HEAD_DIM)
    tok = jnp.concatenate([sub[:, s:s + nc] for s in range(CMP_BLOCK // CMP_STRIDE)], axis=2)
    tok = (tok + pe[:, None, :]).transpose(0, 1, 3, 2, 4).reshape(b, nc, NSA_KV_HEADS, CMP_BLOCK * HEAD_DIM)
    return jax.nn.silu(tok @ w1) @ w2


def nsa_attend(q, gates, qpos, k_c, v_c, k_s, v_s, k_w, v_w, win_base):
    b, t = q.shape[:2]
    g_n, r_n, d = NSA_KV_HEADS, NSA_GROUP, HEAD_DIM
    scale = d ** -0.5
    nc = k_c.shape[1]
    lp = k_s.shape[1]
    nsb = lp // SEL_BLOCK
    nsel = min(N_SEL, nsb)
    ratio = SEL_BLOCK // CMP_STRIDE
    n_sub = CMP_BLOCK // CMP_STRIDE
    cmp_end = jnp.arange(nc, dtype=jnp.int32) * CMP_STRIDE + (CMP_BLOCK - 1)
    blk_id = jnp.arange(nsb, dtype=jnp.int32)
    ks_blk = k_s.reshape(b, nsb, SEL_BLOCK, g_n, d)
    vs_blk = v_s.reshape(b, nsb, SEL_BLOCK, g_n, d)
    kw_pad = jnp.pad(k_w, ((0, 0), (WINDOW, 0), (0, 0), (0, 0)))
    vw_pad = jnp.pad(v_w, ((0, 0), (WINDOW, 0), (0, 0), (0, 0)))
    b_idx = jnp.arange(b)[:, None, None, None]
    g_idx = jnp.arange(g_n)[None, None, :, None]
    cq = math.gcd(t, QBLK)
    n_blk = t // cq
    nw = WINDOW - 1 + cq

    def chunk(args):
        qc, gc, tq = args
        qg = qc.reshape(b, cq, g_n, r_n, d)
        lc = jnp.einsum('bcgrd,bngd->bcgrn', qg, k_c) * scale
        pc = masked_softmax(lc, (cmp_end[None, :] <= tq[:, None])[None, :, None, None, :])
        o_cmp = jnp.einsum('bcgrn,bngd->bcgrd', pc.astype(v_c.dtype), v_c)
        imp = jnp.pad(pc.sum(axis=3), ((0, 0), (0, 0), (0, 0), (n_sub - 1, ratio * nsb - nc)))
        imp = sum(imp[..., m:m + ratio * nsb:ratio] for m in range(ratio + n_sub - 1))
        open_blk = blk_id[None, :] * SEL_BLOCK <= tq[:, None]
        own = tq[:, None] // SEL_BLOCK
        forced = open_blk & ((blk_id[None, :] == 0) | (blk_id[None, :] == own) | (blk_id[None, :] == own - 1))
        score = jnp.where(forced[None, :, None, :], jnp.inf,
                          jnp.where(open_blk[None, :, None, :], imp, -jnp.inf))
        top_val, top_idx = lax.top_k(score, nsel)
        k_sel = ks_blk[b_idx, top_idx, :, g_idx].reshape(b, cq, g_n, nsel * SEL_BLOCK, d)
        v_sel = vs_blk[b_idx, top_idx, :, g_idx].reshape(b, cq, g_n, nsel * SEL_BLOCK, d)
        kpos = top_idx[..., None] * SEL_BLOCK + jnp.arange(SEL_BLOCK, dtype=jnp.int32)
        smask = ((top_val > -jnp.inf)[..., None] & (kpos <= tq[None, :, None, None, None]))
        smask = smask.reshape(b, cq, g_n, 1, nsel * SEL_BLOCK)
        ls = jnp.einsum('bcgrd,bcgkd->bcgrk', qg, k_sel) * scale
        o_slc = jnp.einsum('bcgrk,bcgkd->bcgrd', masked_softmax(ls, smask).astype(v_sel.dtype), v_sel)
        t0 = tq[0]
        start = t0 - win_base + 1
        kw = lax.dynamic_slice_in_dim(kw_pad, start, nw, axis=1)
        vw = lax.dynamic_slice_in_dim(vw_pad, start, nw, axis=1)
        wpos = t0 - (WINDOW - 1) + jnp.arange(nw, dtype=jnp.int32)
        wmask = ((wpos[None, :] <= tq[:, None]) & (wpos[None, :] > tq[:, None] - WINDOW)
                 & (wpos[None, :] >= win_base))
        lw = jnp.einsum('bcgrd,bkgd->bcgrk', qg, kw) * scale
        o_win = jnp.einsum('bcgrk,bkgd->bcgrd',
                           masked_softmax(lw, wmask[None, :, None, None, :]).astype(vw.dtype), vw)
        g = gc.reshape(b, cq, g_n, r_n, 3)
        o = g[..., 0:1] * o_cmp + g[..., 1:2] * o_slc + g[..., 2:3] * o_win
        return o.reshape(b, cq, NSA_HEADS * d)

    out = lax.map(chunk, (q.reshape(b, n_blk, cq, NSA_HEADS, d).swapaxes(0, 1),
                          gates.reshape(b, n_blk, cq, NSA_HEADS, 3).swapaxes(0, 1),
                          qpos.reshape(n_blk, cq)))
    return out.swapaxes(0, 1).reshape(b, t, NSA_HEADS * d)


def moba_attend(q, qpos, k, v):
    b, t, h_n, d = q.shape
    scale = d ** -0.5
    lp = k.shape[1]
    nb = lp // MOBA_BLOCK
    ktop = min(MOBA_TOPK, nb)
    kb = k.reshape(b, nb, MOBA_BLOCK, h_n, d)
    vb = v.reshape(b, nb, MOBA_BLOCK, h_n, d)
    kmean = jnp.mean(kb.astype(jnp.float32), axis=2)
    blk_id = jnp.arange(nb, dtype=jnp.int32)
    b_idx = jnp.arange(b)[:, None, None, None]
    h_idx = jnp.arange(h_n)[None, None, :, None]
    cq = math.gcd(t, QBLK)
    n_blk = t // cq

    def chunk(args):
        qc, tq = args
        own = tq // MOBA_BLOCK
        sc = jnp.einsum('bchd,bnhd->bchn', qc.astype(jnp.float32), kmean)
        past = blk_id[None, :] < own[:, None]
        sc = jnp.where(past[None, :, None, :], sc, -jnp.inf)
        top_val, top_idx = lax.top_k(sc, ktop)
        idx = jnp.concatenate([top_idx, jnp.broadcast_to(own[None, :, None, None], (b, cq, h_n, 1))], axis=-1)
        ok = jnp.concatenate([top_val > -jnp.inf, jnp.ones((b, cq, h_n, 1), dtype=bool)], axis=-1)
        n_keys = (ktop + 1) * MOBA_BLOCK
        kg = kb[b_idx, idx, :, h_idx].reshape(b, cq, h_n, n_keys, d)
        vg = vb[b_idx, idx, :, h_idx].reshape(b, cq, h_n, n_keys, d)
        kpos = idx[..., None] * MOBA_BLOCK + jnp.arange(MOBA_BLOCK, dtype=jnp.int32)
        mask = (ok[..., None] & (kpos <= tq[None, :, None, None, None])).reshape(b, cq, h_n, n_keys)
        logits = jnp.einsum('bchd,bchkd->bchk', qc, kg) * scale
        o = jnp.einsum('bchk,bchkd->bchd', masked_softmax(logits, mask).astype(vg.dtype), vg)
        return o.reshape(b, cq, h_n * d)

    out = lax.map(chunk, (q.reshape(b, n_blk, cq, h_n, d).swapaxes(0, 1), qpos.reshape(n_blk, cq)))
    return out.swapaxes(0, 1).reshape(b, t, h_n * d)


def mix(x, proj, nsa_kv, moba_kv, n_valid, win_kv, win_base, pos,
        pe_k, w_phi_k1, w_phi_k2, pe_v, w_phi_v1, w_phi_v2, w_out_a, w_out_b, w_out):
    gate, q_a, g_a, z_a, q_b, z_b, g_m = proj
    k_c = compress_rows(nsa_kv[:, :, 0], n_valid, pe_k, w_phi_k1, w_phi_k2)
    v_c = compress_rows(nsa_kv[:, :, 1], n_valid, pe_v, w_phi_v1, w_phi_v2)
    o_a = nsa_attend(q_a, g_a, pos, k_c, v_c, nsa_kv[:, :, 2], nsa_kv[:, :, 3],
                     win_kv[:, :, 0], win_kv[:, :, 1], win_base)
    o_b = moba_attend(q_b, pos, moba_kv[:, :, 0], moba_kv[:, :, 1])
    br_a = (o_a * jax.nn.silu(z_a)) @ w_out_a
    br_b = (o_b * jax.nn.silu(z_b)) @ w_out_b
    merged = g_m[:, :, 0] * br_a + g_m[:, :, 1] * br_b
    return x + gate[:, None, :] * (merged @ w_out)


def setup_inputs(seed: int = 0) -> dict:
    key = jax.random.key(seed)
    ks = jax.random.split(key, 24)
    n_pages = PAST_LEN // PAGE_SIZE
    n_used = DEC_BATCH * n_pages
    n_phys = (n_used * 5 + 3) // 4
    wb = min(WINDOW, PAST_LEN)

    def nrm(k, shape, s=1.0):
        return jax.random.normal(k, shape, jnp.float32) * s

    page_table = jax.random.permutation(ks[0], n_phys)[:n_used].reshape(DEC_BATCH, n_pages).astype(jnp.int32)
    return {
        'x_prompt': nrm(ks[1], (BATCH, SEQ, D_MODEL)),
        'x_sample': nrm(ks[2], (DEC_BATCH, DEC_SEQ, D_MODEL)),
        'cache_nsa': nrm(ks[3], (DEPTH, n_phys, PAGE_SIZE, 4, NSA_KV_HEADS, HEAD_DIM)),
        'cache_moba': nrm(ks[4], (DEPTH, n_phys, PAGE_SIZE, 2, MOBA_HEADS, HEAD_DIM)),
        'state_nsa_win': nrm(ks[5], (DEPTH, DEC_BATCH, wb, 2, NSA_KV_HEADS, HEAD_DIM)),
        'page_table': page_table,
        'c_prompt': nrm(ks[6], (BATCH, D_MODEL)),
        'c_sample': nrm(ks[7], (DEC_BATCH, D_MODEL)),
        'w_ada': nrm(ks[8], (DEPTH, D_MODEL, 3 * D_MODEL), 0.5 * D_MODEL ** -0.5),
        'b_ada': nrm(ks[9], (DEPTH, 3 * D_MODEL), 0.01),
        'norm_g': 1.0 + nrm(ks[10], (DEPTH, D_MODEL), 0.01),
        'w_in': nrm(ks[11], (DEPTH, D_MODEL, D_IN), D_MODEL ** -0.5),
        'pe_k': nrm(ks[12], (DEPTH, CMP_BLOCK, HEAD_DIM), 0.1),
        'w_phi_k1': nrm(ks[13], (DEPTH, CMP_BLOCK * HEAD_DIM, PHI_HIDDEN), (CMP_BLOCK * HEAD_DIM) ** -0.5),
        'w_phi_k2': nrm(ks[14], (DEPTH, PHI_HIDDEN, HEAD_DIM), PHI_HIDDEN ** -0.5),
        'pe_v': nrm(ks[15], (DEPTH, CMP_BLOCK, HEAD_DIM), 0.1),
        'w_phi_v1': nrm(ks[16], (DEPTH, CMP_BLOCK * HEAD_DIM, PHI_HIDDEN), (CMP_BLOCK * HEAD_DIM) ** -0.5),
        'w_phi_v2': nrm(ks[17], (DEPTH, PHI_HIDDEN, HEAD_DIM), PHI_HIDDEN ** -0.5),
        'w_out_a': nrm(ks[18], (DEPTH, W_A, D_MODEL), W_A ** -0.5),
        'w_out_b': nrm(ks[19], (DEPTH, W_B, D_MODEL), W_B ** -0.5),
        'w_out': nrm(ks[20], (DEPTH, D_MODEL, D_MODEL), D_MODEL ** -0.5),
        'final_g': 1.0 + nrm(ks[21], (D_MODEL,), 0.01),
    }


def reference(x_prompt, x_sample, cache_nsa, cache_moba, state_nsa_win, page_table, c_prompt, c_sample,
              w_ada, b_ada, norm_g, w_in, pe_k, w_phi_k1, w_phi_k2, pe_v, w_phi_v1, w_phi_v2,
              w_out_a, w_out_b, w_out, final_g):
    t_p = x_prompt.shape[1]
    dec_b, t_s = x_sample.shape[:2]
    past = page_table.shape[1] * cache_nsa.shape[2]
    wb = state_nsa_win.shape[2]
    pos_p = jnp.arange(t_p, dtype=jnp.int32)
    pos_s = past + jnp.arange(t_s, dtype=jnp.int32)
    lp_p = round_up(t_p, ROW_ALIGN)
    l_s = past + t_s
    lp_s = round_up(l_s, ROW_ALIGN)
    x_p, x_s = x_prompt, x_sample
    nsa_p, nsa_s, moba_p, moba_s, win_p, win_s = [], [], [], [], [], []
    for l in range(DEPTH):
        proj_p, nsa_rows_p, moba_rows_p, win_rows_p = project(x_p, c_prompt, pos_p, w_ada[l], b_ada[l],
                                                              norm_g[l], w_in[l])
        x_p = mix(x_p, proj_p, pad_rows(nsa_rows_p, lp_p), pad_rows(moba_rows_p, lp_p), t_p, win_rows_p, 0, pos_p,
                  pe_k[l], w_phi_k1[l], w_phi_k2[l], pe_v[l], w_phi_v1[l], w_phi_v2[l],
                  w_out_a[l], w_out_b[l], w_out[l])
        nsa_p.append(nsa_rows_p)
        moba_p.append(moba_rows_p)
        win_p.append(win_rows_p[:, t_p - min(WINDOW, t_p):])
        proj_s, nsa_rows_s, moba_rows_s, win_rows_s = project(x_s, c_sample, pos_s, w_ada[l], b_ada[l],
                                                              norm_g[l], w_in[l])
        nsa_kv = jnp.concatenate([gather_pages(cache_nsa, l, page_table), nsa_rows_s,
                                  jnp.zeros((dec_b, lp_s - l_s) + nsa_rows_s.shape[2:], nsa_rows_s.dtype)], axis=1)
        moba_kv = jnp.concatenate([gather_pages(cache_moba, l, page_table), moba_rows_s,
                                   jnp.zeros((dec_b, lp_s - l_s) + moba_rows_s.shape[2:], moba_rows_s.dtype)], axis=1)
        win_kv = jnp.concatenate([state_nsa_win[l], win_rows_s], axis=1)
        x_s = mix(x_s, proj_s, nsa_kv, moba_kv, l_s, win_kv, past - wb, pos_s,
                  pe_k[l], w_phi_k1[l], w_phi_k2[l], pe_v[l], w_phi_v1[l], w_phi_v2[l],
                  w_out_a[l], w_out_b[l], w_out[l])
        nsa_s.append(nsa_rows_s)
        moba_s.append(moba_rows_s)
        win_s.append(win_kv[:, t_s:])
    y_prompt = rmsnorm(x_p, final_g)
    y_sample = rmsnorm(x_s, final_g)
    return (y_prompt, y_sample, jnp.stack(nsa_p), jnp.stack(nsa_s), jnp.stack(moba_p), jnp.stack(moba_s),
            jnp.stack(win_p), jnp.stack(win_s))
```

```python
import functools

import numpy as np
import jax
import jax.numpy as jnp
from jax import lax
from jax.experimental import pallas as pl
from jax.experimental.pallas import tpu as pltpu

F32 = jnp.float32
BF16 = jnp.bfloat16

HEAD_DIM = 128
NSA_HEADS = 8
NSA_KV_HEADS = 2
NSA_GROUP = NSA_HEADS // NSA_KV_HEADS
CMP_BLOCK = 32
CMP_STRIDE = 16
SEL_BLOCK = 64
N_SEL = 16
WINDOW = 512
PHI_HIDDEN = 2 * HEAD_DIM
MOBA_HEADS = 8
MOBA_BLOCK = 256
MOBA_TOPK = 3
ROPE_THETA = 10000.0
NORM_EPS = 1e-6

LANES = 128
MASK_BIAS = -(2.0 ** 100)
M_INIT = -1e30
VMEM_LIMIT = 56 * 1024 * 1024

NT_DIMS = (((1,), (1,)), ((), ()))
TN_DIMS = (((0,), (0,)), ((), ()))


def _cparams(sem):
    return pltpu.CompilerParams(dimension_semantics=sem, vmem_limit_bytes=VMEM_LIMIT)


def _sigmoid(x):
    return 1.0 / (1.0 + jnp.exp(-x))


def _ada_kernel(c_ref, w_ref, b_ref, o_ref):
    c = c_ref[...]
    a = (c * _sigmoid(c)).astype(BF16)
    o_ref[...] = jnp.dot(a, w_ref[...].astype(BF16), preferred_element_type=F32) + b_ref[...]


def ada_mod(c, w, b, tn=768):
    m, k = c.shape
    n = w.shape[1]
    return pl.pallas_call(
        _ada_kernel,
        out_shape=jax.ShapeDtypeStruct((m, n), F32),
        grid=(n // tn,),
        in_specs=[pl.BlockSpec((m, k), lambda j: (0, 0)),
                  pl.BlockSpec((k, tn), lambda j: (0, j)),
                  pl.BlockSpec((1, tn), lambda j: (0, j))],
        out_specs=pl.BlockSpec((m, tn), lambda j: (0, j)),
        compiler_params=_cparams(("arbitrary",)),
        name="ada_mod",
    )(c, w, b)


def _h_kernel(x_ref, g_ref, sc_ref, sh_ref, o_ref):
    x = x_ref[...]
    ms = jnp.mean(x * x, axis=-1, keepdims=True)
    h = x * lax.rsqrt(ms + NORM_EPS) * g_ref[...]
    o_ref[...] = (h * (1.0 + sc_ref[...]) + sh_ref[...]).astype(BF16)


def norm_modulate(x, g, scale, shift, tm):
    m, d = x.shape
    per_row = scale.shape[0] != 1
    mod_spec = (pl.BlockSpec((tm, d), lambda i: (i, 0)) if per_row
                else pl.BlockSpec((1, d), lambda i: (0, 0)))
    return pl.pallas_call(
        _h_kernel,
        out_shape=jax.ShapeDtypeStruct((m, d), BF16),
        grid=(m // tm,),
        in_specs=[pl.BlockSpec((tm, d), lambda i: (i, 0)),
                  pl.BlockSpec((1, d), lambda i: (0, 0)),
                  mod_spec, mod_spec],
        out_specs=pl.BlockSpec((tm, d), lambda i: (i, 0)),
        compiler_params=_cparams(("arbitrary",)),
        name="norm_modulate",
    )(x, g, scale, shift)


def _proj_rope_kernel(h_ref, w_ref, cos_ref, sin_ref, of_ref, ob_ref, *, full_tiles, partial):
    acc = jnp.dot(h_ref[...], w_ref[...], preferred_element_type=F32)
    all_heads = pl.program_id(1) < full_tiles
    cos = cos_ref[...]
    sin = sin_ref[...]
    for t in range(acc.shape[1] // LANES):
        a = acc[:, t * LANES:(t + 1) * LANES]
        r = a * cos + pltpu.roll(a, HEAD_DIM // 2, axis=1) * sin
        out = r if t < partial else jnp.where(all_heads, r, a)
        of_ref[:, t * LANES:(t + 1) * LANES] = out
        ob_ref[:, t * LANES:(t + 1) * LANES] = out.astype(BF16)


def proj_rope(h, w, cos2, sin2, tm, tn, full_tiles, partial):
    m, k = h.shape
    n = w.shape[1]
    return pl.pallas_call(
        functools.partial(_proj_rope_kernel, full_tiles=full_tiles, partial=partial),
        out_shape=(jax.ShapeDtypeStruct((m, n), F32), jax.ShapeDtypeStruct((m, n), BF16)),
        grid=(m // tm, n // tn),
        in_specs=[pl.BlockSpec((tm, k), lambda i, j: (i, 0)),
                  pl.BlockSpec((k, tn), lambda i, j: (0, j)),
                  pl.BlockSpec((tm, LANES), lambda i, j: (i, 0)),
                  pl.BlockSpec((tm, LANES), lambda i, j: (i, 0))],
        out_specs=(pl.BlockSpec((tm, tn), lambda i, j: (i, j)),
                   pl.BlockSpec((tm, tn), lambda i, j: (i, j))),
        compiler_params=_cparams(("arbitrary", "arbitrary")),
        name="proj_rope",
    )(h, w, cos2, sin2)


def _proj_act_kernel(h_ref, w_ref, o_ref, *, act):
    acc = jnp.dot(h_ref[...], w_ref[...], preferred_element_type=F32)
    s = _sigmoid(acc)
    o_ref[...] = (acc * s if act == "silu" else s).astype(o_ref.dtype)


def proj_act(h, w, act, tm, tn):
    m, k = h.shape
    n = w.shape[1]
    return pl.pallas_call(
        functools.partial(_proj_act_kernel, act=act),
        out_shape=jax.ShapeDtypeStruct((m, n), F32),
        grid=(m // tm, n // tn),
        in_specs=[pl.BlockSpec((tm, k), lambda i, j: (i, 0)),
                  pl.BlockSpec((k, tn), lambda i, j: (0, j))],
        out_specs=pl.BlockSpec((tm, tn), lambda i, j: (i, j)),
        compiler_params=_cparams(("arbitrary", "arbitrary")),
        name="proj_" + act,
    )(h, w)


def _compress_kernel(s_ref, pe_ref, w1_ref, w2_ref, o_ref):
    s = s_ref[0, 0]
    half = s.shape[1]
    top = jnp.dot((s + pe_ref[0, 0:1]).astype(BF16), w1_ref[0, :half], preferred_element_type=F32)
    bot = jnp.dot((s + pe_ref[0, 1:2]).astype(BF16), w1_ref[0, half:], preferred_element_type=F32)
    hid = top + pltpu.roll(bot, s.shape[0] - 1, axis=0)
    hid = (hid * _sigmoid(hid)).astype(BF16)
    o_ref[0, 0] = jnp.dot(hid, w2_ref[0], preferred_element_type=F32).astype(BF16)


def compress(sub, pe2, w1, w2):
    b, four, n_sub, width = sub.shape
    return pl.pallas_call(
        _compress_kernel,
        out_shape=jax.ShapeDtypeStruct((b, four, n_sub, HEAD_DIM), BF16),
        grid=(four, b),
        in_specs=[pl.BlockSpec((1, 1, n_sub, width), lambda j, i: (i, j, 0, 0)),
                  pl.BlockSpec((1, 2, width), lambda j, i: (j // 2, 0, 0)),
                  pl.BlockSpec((1, 2 * width, PHI_HIDDEN), lambda j, i: (j // 2, 0, 0)),
                  pl.BlockSpec((1, PHI_HIDDEN, HEAD_DIM), lambda j, i: (j // 2, 0, 0))],
        out_specs=pl.BlockSpec((1, 1, n_sub, HEAD_DIM), lambda j, i: (i, j, 0, 0)),
        compiler_params=_cparams(("arbitrary", "arbitrary")),
        name="compress",
    )(sub, pe2, w1, w2)


def _kmean_kernel(k_ref, o_ref):
    k = k_ref[0]
    nb = k.shape[0] // MOBA_BLOCK
    o_ref[0] = jnp.mean(k.reshape(nb, MOBA_BLOCK, k.shape[1]), axis=1)


def block_means(rows, col_block, width, blocks_per_step=8):
    b, t, _ = rows.shape
    nb = t // MOBA_BLOCK
    return pl.pallas_call(
        _kmean_kernel,
        out_shape=jax.ShapeDtypeStruct((b, nb, width), F32),
        grid=(b, nb // blocks_per_step),
        in_specs=[pl.BlockSpec((1, blocks_per_step * MOBA_BLOCK, width),
                               lambda i, j: (i, j, col_block))],
        out_specs=pl.BlockSpec((1, blocks_per_step, width), lambda i, j: (i, j, 0)),
        compiler_params=_cparams(("arbitrary", "arbitrary")),
        name="block_means",
    )(rows)


def _masked_softmax_cols(s, valid):
    s = jnp.where(valid, s, -jnp.inf)
    m = jnp.max(s, axis=0, keepdims=True)
    m = jnp.where(m > -jnp.inf, m, 0.0)
    p = jnp.exp(s - m)
    return p / jnp.maximum(jnp.sum(p, axis=0, keepdims=True), 1e-30)


def _transpose(x):
    rows, cols = x.shape
    pr, pc = -rows % LANES, -cols % LANES
    if pr:
        x = jnp.concatenate([x, jnp.zeros((pr, cols), x.dtype)], axis=0)
    if pc:
        x = jnp.concatenate([x, jnp.zeros((rows + pr, pc), x.dtype)], axis=1)
    return x.T[:cols, :rows]


def _select_top(score, blk, n_iter, limit):
    sel = jnp.zeros(score.shape, jnp.bool_)
    s = score
    blk = blk.astype(F32)
    n_blk = float(score.shape[0])
    for it in range(n_iter):
        mx = jnp.max(s, axis=0, keepdims=True)
        idx = jnp.min(jnp.where(s == mx, blk, n_blk), axis=0, keepdims=True)
        pick = blk == idx
        ok = mx > -jnp.inf
        if limit is not None:
            ok = ok & (limit > it)
        sel = sel | (pick & ok)
        s = jnp.where(pick, -jnp.inf, s)
    return sel


def _block_sparse_flash(qp, k_ref, v_ref, tq_lane, cs, tk, blk_shift, scale):
    r = qp.shape[0]
    row = lax.broadcasted_iota(jnp.int32, (tk, LANES), 0)
    lane = lax.broadcasted_iota(jnp.int32, (tk, LANES), 1)
    rel_blk = lane - (row >> blk_shift)
    blocks_per_tile = tk >> blk_shift

    def scores(i):
        start = pl.multiple_of(i * tk, tk)
        k_t = k_ref[0, pl.ds(start, tk), :]
        onehot = jnp.where(rel_blk == i * blocks_per_tile, 1.0, 0.0).astype(BF16)
        kp = jnp.concatenate([k_t, onehot], axis=1)
        return lax.dot_general(kp, qp, NT_DIMS, preferred_element_type=F32) * scale

    def update(carry, s, i):
        m, l, acc = carry
        m_new = jnp.maximum(m, jnp.max(s, axis=0, keepdims=True))
        alpha = jnp.exp(m - m_new)
        p = jnp.exp(s - m_new)
        l = alpha * l + jnp.sum(p, axis=0, keepdims=True)
        start = pl.multiple_of(i * tk, tk)
        v_t = v_ref[0, pl.ds(start, tk), :]
        pv = lax.dot_general(v_t, p.astype(BF16), TN_DIMS, preferred_element_type=F32)
        return m_new, l, alpha * acc + pv

    def body(i, carry):
        return update(carry, scores(i), i)

    init = (jnp.full((1, r), M_INIT, F32), jnp.zeros((1, r), F32), jnp.zeros((HEAD_DIM, r), F32))
    n_full = cs // tk
    carry = lax.fori_loop(0, n_full, body, init)
    kpos = n_full * tk + lax.broadcasted_iota(jnp.int32, (tk, 1), 0)
    s = jnp.where(kpos <= tq_lane, scores(n_full), M_INIT)
    _, l, acc = update(carry, s, n_full)
    return acc / l


def _nsa_kernel(q_ref, ga_ref, z_ref, kc_ref, vc_ref, ks_ref, vs_ref, kw_ref, vw_ref, pool_ref,
                o_ref, *, cq, tk, qpos0, n_cmp, win_base):
    c = pl.program_id(2)
    r = NSA_GROUP * cq
    cs = qpos0 + c * cq
    scale = HEAD_DIM ** -0.5
    q = q_ref[0]
    qs = jnp.concatenate([q[:, h * LANES:(h + 1) * LANES] for h in range(NSA_GROUP)], axis=0)
    tq_lane = cs + (lax.broadcasted_iota(jnp.int32, (1, r), 1) & (cq - 1))

    kc = kc_ref[0, 0]
    nc_pad = kc.shape[0]
    n_id = lax.broadcasted_iota(jnp.int32, (nc_pad, 1), 0)
    sc = lax.dot_general(kc, qs, NT_DIMS, preferred_element_type=F32) * scale
    valid = ((n_id * CMP_STRIDE + (CMP_BLOCK - 1)) <= tq_lane) & (n_id < n_cmp)
    pc = _masked_softmax_cols(sc, valid)
    o_cmp = lax.dot_general(vc_ref[0, 0], pc.astype(BF16), TN_DIMS, preferred_element_type=F32)

    psum = pc[:, 0:cq]
    for h in range(1, NSA_GROUP):
        psum = psum + pc[:, h * cq:(h + 1) * cq]
    hi = psum.astype(BF16)
    r1 = psum - hi.astype(F32)
    mid = r1.astype(BF16)
    lo = (r1 - mid.astype(F32)).astype(BF16)
    imp = jnp.dot(pool_ref[...], jnp.concatenate([hi, mid, lo], axis=0), preferred_element_type=F32)

    blk = lax.broadcasted_iota(jnp.int32, (LANES, 1), 0)
    tq = cs + lax.broadcasted_iota(jnp.int32, (1, cq), 1)
    own = tq >> 6
    open_blk = blk * SEL_BLOCK <= tq
    forced = open_blk & ((blk == 0) | (blk == own) | (blk == own - 1))
    score = jnp.where(forced, jnp.inf, jnp.where(open_blk, imp, -jnp.inf))
    n_beyond = (own >= LANES).astype(jnp.int32) + (own - 1 >= LANES).astype(jnp.int32)
    sel = _select_top(score, blk, N_SEL, N_SEL - n_beyond)
    bias = _transpose(jnp.where(sel, 0.0, MASK_BIAS)).astype(BF16)
    qp = jnp.concatenate([qs, jnp.concatenate([bias] * NSA_GROUP, axis=0)], axis=1)
    o_slc = _block_sparse_flash(qp, ks_ref, vs_ref, tq_lane, cs, tk, 6, scale)

    tw = WINDOW + cq
    w_start = pl.multiple_of(c * cq, cq)
    k_w = kw_ref[0, pl.ds(w_start, tw), :]
    v_w = vw_ref[0, pl.ds(w_start, tw), :]
    wpos = (cs - WINDOW) + lax.broadcasted_iota(jnp.int32, (tw, 1), 0)
    sw = lax.dot_general(k_w, qs, NT_DIMS, preferred_element_type=F32) * scale
    wvalid = (wpos <= tq_lane) & (wpos > tq_lane - WINDOW) & (wpos >= win_base)
    pw = _masked_softmax_cols(sw, wvalid)
    o_win = lax.dot_general(v_w, pw.astype(BF16), TN_DIMS, preferred_element_type=F32)

    g = _transpose(ga_ref[0])
    for h in range(NSA_GROUP):
        cols = slice(h * cq, (h + 1) * cq)
        o = (g[3 * h:3 * h + 1] * o_cmp[:, cols] + g[3 * h + 1:3 * h + 2] * o_slc[:, cols]
             + g[3 * h + 2:3 * h + 3] * o_win[:, cols])
        o_ref[0, :, h * LANES:(h + 1) * LANES] = (
            _transpose(o) * z_ref[0, :, h * LANES:(h + 1) * LANES]).astype(BF16)


def nsa_attention(q, q_col, gates, z, z_col, kc, ks, ks_col, vs_col, kw, kw_col, vw_col, pool,
                  *, cq, tk, qpos0, n_cmp, win_base):
    b, tq_len, _ = q.shape
    tkv = ks.shape[1]
    tw_len = kw.shape[1]
    nc_pad = kc.shape[2]
    gw = NSA_GROUP * LANES
    kern = functools.partial(_nsa_kernel, cq=cq, tk=tk, qpos0=qpos0, n_cmp=n_cmp, win_base=win_base)
    return pl.pallas_call(
        kern,
        out_shape=jax.ShapeDtypeStruct((b, tq_len, NSA_HEADS * HEAD_DIM), BF16),
        grid=(b, NSA_KV_HEADS, tq_len // cq),
        in_specs=[
            pl.BlockSpec((1, cq, gw), lambda i, g, c: (i, c, q_col + g)),
            pl.BlockSpec((1, cq, LANES), lambda i, g, c: (i, c, g)),
            pl.BlockSpec((1, cq, gw), lambda i, g, c: (i, c, z_col + g)),
            pl.BlockSpec((1, 1, nc_pad, HEAD_DIM), lambda i, g, c: (i, g, 0, 0)),
            pl.BlockSpec((1, 1, nc_pad, HEAD_DIM), lambda i, g, c: (i, 2 + g, 0, 0)),
            pl.BlockSpec((1, tkv, HEAD_DIM), lambda i, g, c: (i, 0, ks_col + g)),
            pl.BlockSpec((1, tkv, HEAD_DIM), lambda i, g, c: (i, 0, vs_col + g)),
            pl.BlockSpec((1, tw_len, HEAD_DIM), lambda i, g, c: (i, 0, kw_col + g)),
            pl.BlockSpec((1, tw_len, HEAD_DIM), lambda i, g, c: (i, 0, vw_col + g)),
            pl.BlockSpec(pool.shape, lambda i, g, c: (0, 0)),
        ],
        out_specs=pl.BlockSpec((1, cq, gw), lambda i, g, c: (i, c, g)),
        compiler_params=_cparams(("arbitrary", "arbitrary", "arbitrary")),
        name="nsa_attention",
    )(q, gates, z, kc, kc, ks, ks, kw, kw, pool)


def _moba_kernel(q_ref, z_ref, km_ref, k_ref, v_ref, o_ref, *, cq, tk, qpos0):
    c = pl.program_id(2)
    cs = qpos0 + c * cq
    scale = HEAD_DIM ** -0.5
    q = q_ref[0]
    tq = cs + lax.broadcasted_iota(jnp.int32, (1, cq), 1)
    blk = lax.broadcasted_iota(jnp.int32, (LANES, 1), 0)
    own = tq >> 8
    sc = lax.dot_general(km_ref[0], q, NT_DIMS, preferred_element_type=F32)
    score = jnp.where(blk < own, sc, -jnp.inf)
    sel = _select_top(score, blk, MOBA_TOPK, None) | (blk == own)
    bias = _transpose(jnp.where(sel, 0.0, MASK_BIAS)).astype(BF16)
    qp = jnp.concatenate([q, bias], axis=1)
    o = _block_sparse_flash(qp, k_ref, v_ref, tq, cs, tk, 8, scale)
    o_ref[0] = (_transpose(o) * z_ref[0]).astype(BF16)


def moba_attention(q, q_col, z, z_col, kmean, k, k_col, v_col, *, cq, tk, qpos0):
    b, tq_len, _ = q.shape
    tkv = k.shape[1]
    kern = functools.partial(_moba_kernel, cq=cq, tk=tk, qpos0=qpos0)
    return pl.pallas_call(
        kern,
        out_shape=jax.ShapeDtypeStruct((b, tq_len, MOBA_HEADS * HEAD_DIM), BF16),
        grid=(b, MOBA_HEADS, tq_len // cq),
        in_specs=[
            pl.BlockSpec((1, cq, LANES), lambda i, h, c: (i, c, q_col + h)),
            pl.BlockSpec((1, cq, LANES), lambda i, h, c: (i, c, z_col + h)),
            pl.BlockSpec((1, LANES, HEAD_DIM), lambda i, h, c: (i, 0, h)),
            pl.BlockSpec((1, tkv, HEAD_DIM), lambda i, h, c: (i, 0, k_col + h)),
            pl.BlockSpec((1, tkv, HEAD_DIM), lambda i, h, c: (i, 0, v_col + h)),
        ],
        out_specs=pl.BlockSpec((1, cq, LANES), lambda i, h, c: (i, c, h)),
        compiler_params=_cparams(("arbitrary", "arbitrary", "arbitrary")),
        name="moba_attention",
    )(q, z, kmean, k, k)


def _merge_kernel(oa_ref, ob_ref, ga_ref, gb_ref, wa_ref, wb_ref, o_ref):
    br_a = jnp.dot(oa_ref[...], wa_ref[...], preferred_element_type=F32)
    br_b = jnp.dot(ob_ref[...], wb_ref[...], preferred_element_type=F32)
    o_ref[...] = (ga_ref[...] * br_a + gb_ref[...] * br_b).astype(BF16)


def merge_branches(oz_a, oz_b, g_m, w_a, w_b, tm):
    m, ka = oz_a.shape
    d = w_a.shape[1]
    return pl.pallas_call(
        _merge_kernel,
        out_shape=jax.ShapeDtypeStruct((m, d), BF16),
        grid=(m // tm,),
        in_specs=[pl.BlockSpec((tm, ka), lambda i: (i, 0)),
                  pl.BlockSpec((tm, ka), lambda i: (i, 0)),
                  pl.BlockSpec((tm, d), lambda i: (i, 0)),
                  pl.BlockSpec((tm, d), lambda i: (i, 1)),
                  pl.BlockSpec((ka, d), lambda i: (0, 0)),
                  pl.BlockSpec((ka, d), lambda i: (0, 0))],
        out_specs=pl.BlockSpec((tm, d), lambda i: (i, 0)),
        compiler_params=_cparams(("arbitrary",)),
        name="merge_branches",
    )(oz_a, oz_b, g_m, g_m, w_a, w_b)


def _final_kernel(m_ref, x_ref, gate_ref, w_ref, g_ref, o_ref):
    y = x_ref[...] + gate_ref[...] * jnp.dot(m_ref[...], w_ref[...], preferred_element_type=F32)
    ms = jnp.mean(y * y, axis=-1, keepdims=True)
    o_ref[...] = y * lax.rsqrt(ms + NORM_EPS) * g_ref[...]


def final_out(merged, x, gate, w, g, tm):
    m, d = x.shape
    per_row = gate.shape[0] != 1
    gate_spec = (pl.BlockSpec((tm, d), lambda i: (i, 0)) if per_row
                 else pl.BlockSpec((1, d), lambda i: (0, 0)))
    return pl.pallas_call(
        _final_kernel,
        out_shape=jax.ShapeDtypeStruct((m, d), F32),
        grid=(m // tm,),
        in_specs=[pl.BlockSpec((tm, d), lambda i: (i, 0)),
                  pl.BlockSpec((tm, d), lambda i: (i, 0)),
                  gate_spec,
                  pl.BlockSpec((d, d), lambda i: (0, 0)),
                  pl.BlockSpec((1, d), lambda i: (0, 0))],
        out_specs=pl.BlockSpec((tm, d), lambda i: (i, 0)),
        compiler_params=_cparams(("arbitrary",)),
        name="final_out",
    )(merged, x, gate, w, g)


def _gather_nsa_kernel(pt_ref, page_ref, new_ref, cmp_ref, slc_ref, *, n_pages):
    p = pl.program_id(1)

    @pl.when(p < n_pages)
    def _():
        page = page_ref[0]
        for j in range(4):
            cmp_ref[0, j] = page[:, j * LANES:(j + 1) * LANES]
        slc_ref[0] = page[:, 4 * LANES:].astype(BF16)

    @pl.when(p >= n_pages)
    def _():
        slc_ref[0] = new_ref[0]


def gather_nsa(cache, page_table, new_rows):
    b, n_pages = page_table.shape
    page = cache.shape[1]
    tail_pages = new_rows.shape[1] // page
    half = 4 * LANES
    grid_spec = pltpu.PrefetchScalarGridSpec(
        num_scalar_prefetch=1,
        grid=(b, n_pages + tail_pages),
        in_specs=[
            pl.BlockSpec((1, page, 2 * half),
                         lambda i, p, pt: (pt[i, jnp.minimum(p, n_pages - 1)], 0, 0)),
            pl.BlockSpec((1, page, half), lambda i, p, pt: (i, jnp.maximum(p - n_pages, 0), 0)),
        ],
        out_specs=(
            pl.BlockSpec((1, 4, page, LANES), lambda i, p, pt: (i, 0, jnp.minimum(p, n_pages - 1), 0)),
            pl.BlockSpec((1, page, half), lambda i, p, pt: (i, p, 0)),
        ),
    )
    return pl.pallas_call(
        functools.partial(_gather_nsa_kernel, n_pages=n_pages),
        out_shape=(jax.ShapeDtypeStruct((b, 4, n_pages * page, LANES), F32),
                   jax.ShapeDtypeStruct((b, (n_pages + tail_pages) * page, half), BF16)),
        grid_spec=grid_spec,
        compiler_params=_cparams(("arbitrary", "arbitrary")),
        name="gather_nsa",
    )(page_table, cache, new_rows)


def _gather_moba_kernel(pt_ref, pa_ref, pb_ref, new_ref, kv_ref, km_ref, *, n_blocks, page):
    p = pl.program_id(1)
    kw = MOBA_HEADS * HEAD_DIM

    @pl.when(p < n_blocks)
    def _():
        a = pa_ref[0]
        bq = pb_ref[0]
        kv_ref[0, :page] = a.astype(BF16)
        kv_ref[0, page:] = bq.astype(BF16)
        tot = jnp.sum(a[:, :kw], axis=0, keepdims=True) + jnp.sum(bq[:, :kw], axis=0, keepdims=True)
        km_ref[0, 0] = tot / MOBA_BLOCK

    @pl.when(p >= n_blocks)
    def _():
        kv_ref[0] = new_ref[0]
        km_ref[0, 0] = jnp.zeros((1, kw), F32)


def gather_moba(cache, page_table, new_rows):
    b, n_pages = page_table.shape
    page = cache.shape[1]
    width = cache.shape[2]
    n_blocks = n_pages * page // MOBA_BLOCK
    last = n_pages - 1
    grid_spec = pltpu.PrefetchScalarGridSpec(
        num_scalar_prefetch=1,
        grid=(b, n_blocks + 1),
        in_specs=[
            pl.BlockSpec((1, page, width), lambda i, p, pt: (pt[i, jnp.minimum(2 * p, last)], 0, 0)),
            pl.BlockSpec((1, page, width), lambda i, p, pt: (pt[i, jnp.minimum(2 * p + 1, last)], 0, 0)),
            pl.BlockSpec((1, MOBA_BLOCK, width), lambda i, p, pt: (i, 0, 0)),
        ],
        out_specs=(
            pl.BlockSpec((1, MOBA_BLOCK, width), lambda i, p, pt: (i, p, 0)),
            pl.BlockSpec((1, 1, 1, width // 2), lambda i, p, pt: (i, p, 0, 0)),
        ),
    )
    return pl.pallas_call(
        functools.partial(_gather_moba_kernel, n_blocks=n_blocks, page=page),
        out_shape=(jax.ShapeDtypeStruct((b, (n_blocks + 1) * MOBA_BLOCK, width), BF16),
                   jax.ShapeDtypeStruct((b, n_blocks + 1, 1, width // 2), F32)),
        grid_spec=grid_spec,
        compiler_params=_cparams(("arbitrary", "arbitrary")),
        name="gather_moba",
    )(page_table, cache, cache, new_rows)


def _rope_tables(pos):
    half = HEAD_DIM // 2
    inv = ROPE_THETA ** (-(jnp.arange(half, dtype=F32) / half))
    ang = pos.astype(F32)[:, None] * inv[None, :]
    cos, sin = jnp.cos(ang), jnp.sin(ang)
    return jnp.concatenate([cos, cos], axis=1), jnp.concatenate([-sin, sin], axis=1)


def _pool_matrix(nc_pad, n_cmp):
    ratio = SEL_BLOCK // CMP_STRIDE
    n_sub = CMP_BLOCK // CMP_STRIDE
    j = np.arange(LANES)[:, None]
    n = np.arange(nc_pad)[None, :]
    pool = ((n >= ratio * j - (n_sub - 1)) & (n <= ratio * j + ratio - 1) & (n < n_cmp)).astype(np.float32)
    return jnp.asarray(np.concatenate([pool, pool, pool], axis=1), dtype=BF16)


def _pad_rows(a, rows):
    return jnp.pad(a, ((0, 0), (0, rows - a.shape[1]), (0, 0)))


def kernel(x_prompt, x_sample, cache_nsa, cache_moba, state_nsa_win, page_table, c_prompt, c_sample,
           w_ada, b_ada, norm_g, w_in, pe_k, w_phi_k1, w_phi_k2, pe_v, w_phi_v1, w_phi_v2,
           w_out_a, w_out_b, w_out, final_g):
    depth = w_in.shape[0]
    assert depth == 1, "single-layer step"
    d_model = x_prompt.shape[-1]
    bp, t_p, _ = x_prompt.shape
    bs, t_s, _ = x_sample.shape
    assert bp == 1 and t_s == 1
    page = cache_nsa.shape[2]
    n_pages = page_table.shape[1]
    past = n_pages * page
    wb = state_nsa_win.shape[2]
    w_a = NSA_HEADS * HEAD_DIM
    w_b = MOBA_HEADS * HEAD_DIM
    kv_a = NSA_KV_HEADS * HEAD_DIM
    s0 = w_a + 6 * kv_a
    s1 = s0 + 3 * NSA_HEADS
    s2 = s1 + w_a
    s3 = s2 + 3 * w_b
    s4 = s3 + w_b

    wi = w_in[0]
    w_nsa = wi[:, :s0].astype(BF16)
    w_moba = wi[:, s2:s3].astype(BF16)
    w_z = jnp.concatenate([wi[:, s1:s2], wi[:, s3:s4]], axis=1).astype(BF16)
    w_gm = wi[:, s4:].astype(BF16)
    w_ga = wi[:, s0:s1].reshape(d_model, NSA_KV_HEADS, 3 * NSA_GROUP)
    w_ga = jnp.pad(w_ga, ((0, 0), (0, 0), (0, LANES - 3 * NSA_GROUP))).reshape(d_model, -1).astype(BF16)
    w1 = jnp.stack([w_phi_k1[0], w_phi_v1[0]]).astype(BF16)
    w2 = jnp.stack([w_phi_k2[0], w_phi_v2[0]]).astype(BF16)
    half_w = CMP_STRIDE * HEAD_DIM
    pe2 = jnp.stack([pe_k[0].reshape(2, half_w), pe_v[0].reshape(2, half_w)])
    woa = w_out_a[0].astype(BF16)
    wob = w_out_b[0].astype(BF16)
    wo = w_out[0].astype(BF16)
    fg = final_g.reshape(1, d_model)
    ng = norm_g[0].reshape(1, d_model)

    c_all = jnp.concatenate([c_prompt, c_sample], axis=0)
    n_c = c_all.shape[0]
    c_all = jnp.pad(c_all, ((0, -n_c % 8), (0, 0)))
    mod = ada_mod(c_all, w_ada[0], b_ada[0].reshape(1, -1))
    shift, scale, gate = mod[:, :d_model], mod[:, d_model:2 * d_model], mod[:, 2 * d_model:]

    def project(x2d, rows, pos, tm):
        sl = slice(rows.start, rows.stop)
        h = norm_modulate(x2d, ng, scale[sl], shift[sl], tm)
        cos2, sin2 = _rope_tables(pos)
        nsa_f, nsa_b = proj_rope(h, w_nsa, cos2, sin2, tm, 512, w_a // 512, 2)
        moba_f, moba_b = proj_rope(h, w_moba, cos2, sin2, tm, 512, 2 * w_b // 512, 0)
        z = proj_act(h, w_z, "silu", tm, 512)
        g_m = proj_act(h, w_gm, "sigmoid", tm, 512)
        g_a = proj_act(h, w_ga, "sigmoid", tm, NSA_KV_HEADS * LANES)
        return nsa_f, nsa_b, moba_f, moba_b, z, g_m, g_a

    def finish(oz_a, oz_b, g_m, x2d, rows, tm):
        merged = merge_branches(oz_a, oz_b, g_m, woa, wob, tm)
        return final_out(merged, x2d, gate[rows.start:rows.stop], wo, fg, tm)

    assert t_p % MOBA_BLOCK == 0
    xp = x_prompt.reshape(t_p, d_model)
    tm_p = 1024 if t_p % 1024 == 0 else 256
    nsa_f, nsa_b, moba_f, moba_b, z_p, gm_p, ga_p = project(
        xp, slice(0, 1), jnp.arange(t_p, dtype=jnp.int32), tm_p)
    n_cmp_p = (t_p - CMP_BLOCK) // CMP_STRIDE + 1
    n_sub_p = t_p // CMP_STRIDE
    assert n_cmp_p == n_sub_p - 1 and n_sub_p % 8 == 0
    sub_p = nsa_f[:, w_a:w_a + 2 * kv_a].reshape(n_sub_p, CMP_STRIDE, 4, HEAD_DIM)
    sub_p = sub_p.transpose(2, 0, 1, 3).reshape(1, 4, n_sub_p, half_w)
    kc_p = compress(sub_p, pe2, w1, w2)
    pool_p = _pool_matrix(n_sub_p, n_cmp_p)
    cq_p = 128
    tk_p = 512 if t_p % 512 == 0 else 256
    assert (t_p - 1) // SEL_BLOCK < LANES
    win_p = jnp.pad(nsa_b[:, w_a + 4 * kv_a:], ((WINDOW, 0), (0, 0)))[None]
    nsa_b3 = nsa_b[None]
    oz_a = nsa_attention(
        nsa_b3, 0, ga_p[None], z_p[None], 0, kc_p,
        nsa_b3, (w_a + 2 * kv_a) // LANES, (w_a + 3 * kv_a) // LANES,
        win_p, 0, NSA_KV_HEADS, pool_p,
        cq=cq_p, tk=tk_p, qpos0=0, n_cmp=n_cmp_p, win_base=0)
    moba_f3 = moba_f[None]
    moba_b3 = moba_b[None]
    km_p = block_means(moba_f3, 1, w_b, blocks_per_step=8 if (t_p // MOBA_BLOCK) % 8 == 0 else t_p // MOBA_BLOCK)
    km_p = _pad_rows(km_p, LANES).astype(BF16)
    cqm_p = 256
    oz_b = moba_attention(moba_b3, 0, z_p[None], w_a // LANES, km_p, moba_b3, w_b // LANES,
                          2 * w_b // LANES, cq=cqm_p, tk=tk_p, qpos0=0)
    y_prompt = finish(oz_a[0], oz_b[0], gm_p, xp, slice(0, 1), 256).reshape(1, t_p, d_model)
    new_nsa_prompt = nsa_f[:, w_a:w_a + 4 * kv_a].reshape(1, 1, t_p, 4, NSA_KV_HEADS, HEAD_DIM)
    new_moba_prompt = moba_f[:, w_b:].reshape(1, 1, t_p, 2, MOBA_HEADS, HEAD_DIM)
    wlen = min(WINDOW, t_p)
    new_win_prompt = nsa_f[t_p - wlen:, w_a + 4 * kv_a:].reshape(1, 1, wlen, 2, NSA_KV_HEADS, HEAD_DIM)

    xs = x_sample.reshape(bs, d_model)
    pos_s = jnp.full((bs,), past, jnp.int32)
    nsa_fs, nsa_bs, moba_fs, moba_bs, z_s, gm_s, ga_s = project(xs, slice(1, 1 + bs), pos_s, bs)
    l_s = past + 1
    lp_s = -(-l_s // MOBA_BLOCK) * MOBA_BLOCK
    tail = lp_s - past
    n_cmp_s = (l_s - CMP_BLOCK) // CMP_STRIDE + 1
    n_sub_s = past // CMP_STRIDE
    assert n_cmp_s == n_sub_s - 1 and n_sub_s % 8 == 0 and tail % page == 0
    assert past % MOBA_BLOCK == 0 and wb == WINDOW and past >= WINDOW
    assert past // SEL_BLOCK <= LANES + 1 and (past // SEL_BLOCK >= LANES or lp_s // SEL_BLOCK <= LANES)

    cache_n = cache_nsa[0].reshape(cache_nsa.shape[1], page, -1)
    cache_m = cache_moba[0].reshape(cache_moba.shape[1], page, -1)
    new_slc = _pad_rows(nsa_bs[:, None, w_a + 2 * kv_a:w_a + 4 * kv_a], tail)
    cmp_rows, slc_s = gather_nsa(cache_n, page_table, new_slc)
    new_kv = _pad_rows(moba_bs[:, None, w_b:], tail)
    kv_s, km_s = gather_moba(cache_m, page_table, new_kv)

    kc_s = compress(cmp_rows.reshape(bs, 4, n_sub_s, half_w), pe2, w1, w2)
    pool_s = _pool_matrix(n_sub_s, n_cmp_s)
    cq_s = 32
    win_rows = jnp.concatenate([state_nsa_win[0].reshape(bs, wb, -1),
                                nsa_fs[:, None, w_a + 4 * kv_a:]], axis=1)
    win_s = _pad_rows(win_rows.astype(BF16), WINDOW + cq_s)

    def pad_q(a, rows):
        return _pad_rows(a[:, None, :], rows)

    oz_as = nsa_attention(
        pad_q(nsa_bs, cq_s), 0, pad_q(ga_s, cq_s), pad_q(z_s, cq_s), 0, kc_s,
        slc_s, 0, NSA_KV_HEADS, win_s, 0, NSA_KV_HEADS, pool_s,
        cq=cq_s, tk=MOBA_BLOCK, qpos0=past, n_cmp=n_cmp_s, win_base=past - wb)
    cqm_s = 128
    km_s = _pad_rows(km_s.reshape(bs, -1, w_b), LANES).astype(BF16)
    oz_bs = moba_attention(pad_q(moba_bs, cqm_s), 0, pad_q(z_s, cqm_s), w_a // LANES, km_s,
                           kv_s, 0, MOBA_HEADS, cq=cqm_s, tk=MOBA_BLOCK, qpos0=past)
    y_sample = finish(oz_as[:, 0], oz_bs[:, 0], gm_s, xs, slice(1, 1 + bs), bs).reshape(bs, 1, d_model)
    new_nsa_sample = nsa_fs[:, w_a:w_a + 4 * kv_a].reshape(1, bs, 1, 4, NSA_KV_HEADS, HEAD_DIM)
    new_moba_sample = moba_fs[:, w_b:].reshape(1, bs, 1, 2, MOBA_HEADS, HEAD_DIM)
    new_win_sample = win_rows[:, 1:].reshape(1, bs, wb, 2, NSA_KV_HEADS, HEAD_DIM)

    return (y_prompt, y_sample, new_nsa_prompt, new_nsa_sample, new_moba_prompt, new_moba_sample,
            new_win_prompt, new_win_sample)
```

```python
import functools

import numpy as np
import jax
import jax.numpy as jnp
from jax import lax
from jax.experimental import pallas as pl
from jax.experimental.pallas import tpu as pltpu

F32 = jnp.float32
BF16 = jnp.bfloat16

HEAD_DIM = 128
NSA_HEADS = 8
NSA_KV_HEADS = 2
NSA_GROUP = NSA_HEADS // NSA_KV_HEADS
CMP_BLOCK = 32
CMP_STRIDE = 16
SEL_BLOCK = 64
N_SEL = 16
WINDOW = 512
PHI_HIDDEN = 2 * HEAD_DIM
MOBA_HEADS = 8
MOBA_BLOCK = 256
MOBA_TOPK = 3
ROPE_THETA = 10000.0
NORM_EPS = 1e-6

LANES = 128
MASK_BIAS = -(2.0 ** 100)
M_INIT = -1e30
VMEM_LIMIT = 56 * 1024 * 1024

NT_DIMS = (((1,), (1,)), ((), ()))
TN_DIMS = (((0,), (0,)), ((), ()))


def _cparams(sem):
    return pltpu.CompilerParams(dimension_semantics=sem, vmem_limit_bytes=VMEM_LIMIT)


def _sigmoid(x):
    return 1.0 / (1.0 + jnp.exp(-x))


def _ada_kernel(c_ref, w_ref, b_ref, o_ref):
    c = c_ref[...]
    a = (c * _sigmoid(c)).astype(BF16)
    o_ref[...] = jnp.dot(a, w_ref[...].astype(BF16), preferred_element_type=F32) + b_ref[...]


def ada_mod(c, w, b, tn=768):
    m, k = c.shape
    n = w.shape[1]
    return pl.pallas_call(
        _ada_kernel,
        out_shape=jax.ShapeDtypeStruct((m, n), F32),
        grid=(n // tn,),
        in_specs=[pl.BlockSpec((m, k), lambda j: (0, 0)),
                  pl.BlockSpec((k, tn), lambda j: (0, j)),
                  pl.BlockSpec((1, tn), lambda j: (0, j))],
        out_specs=pl.BlockSpec((m, tn), lambda j: (0, j)),
        compiler_params=_cparams(("arbitrary",)),
        name="ada_mod",
    )(c, w, b)


def _h_kernel(x_ref, g_ref, sc_ref, sh_ref, o_ref):
    x = x_ref[...]
    ms = jnp.mean(x * x, axis=-1, keepdims=True)
    h = x * lax.rsqrt(ms + NORM_EPS) * g_ref[...]
    o_ref[...] = (h * (1.0 + sc_ref[...]) + sh_ref[...]).astype(BF16)


def norm_modulate(x, g, scale, shift, tm):
    m, d = x.shape
    per_row = scale.shape[0] != 1
    mod_spec = (pl.BlockSpec((tm, d), lambda i: (i, 0)) if per_row
                else pl.BlockSpec((1, d), lambda i: (0, 0)))
    return pl.pallas_call(
        _h_kernel,
        out_shape=jax.ShapeDtypeStruct((m, d), BF16),
        grid=(m // tm,),
        in_specs=[pl.BlockSpec((tm, d), lambda i: (i, 0)),
                  pl.BlockSpec((1, d), lambda i: (0, 0)),
                  mod_spec, mod_spec],
        out_specs=pl.BlockSpec((tm, d), lambda i: (i, 0)),
        compiler_params=_cparams(("arbitrary",)),
        name="norm_modulate",
    )(x, g, scale, shift)


def _proj_rope_kernel(h_ref, w_ref, cos_ref, sin_ref, of_ref, ob_ref, *, full_tiles, partial):
    acc = jnp.dot(h_ref[...], w_ref[...], preferred_element_type=F32)
    all_heads = pl.program_id(1) < full_tiles
    cos = cos_ref[...]
    sin = sin_ref[...]
    for t in range(acc.shape[1] // LANES):
        a = acc[:, t * LANES:(t + 1) * LANES]
        r = a * cos + pltpu.roll(a, HEAD_DIM // 2, axis=1) * sin
        out = r if t < partial else jnp.where(all_heads, r, a)
        of_ref[:, t * LANES:(t + 1) * LANES] = out
        ob_ref[:, t * LANES:(t + 1) * LANES] = out.astype(BF16)


def proj_rope(h, w, cos2, sin2, tm, tn, full_tiles, partial):
    m, k = h.shape
    n = w.shape[1]
    return pl.pallas_call(
        functools.partial(_proj_rope_kernel, full_tiles=full_tiles, partial=partial),
        out_shape=(jax.ShapeDtypeStruct((m, n), F32), jax.ShapeDtypeStruct((m, n), BF16)),
        grid=(m // tm, n // tn),
        in_specs=[pl.BlockSpec((tm, k), lambda i, j: (i, 0)),
                  pl.BlockSpec((k, tn), lambda i, j: (0, j)),
                  pl.BlockSpec((tm, LANES), lambda i, j: (i, 0)),
                  pl.BlockSpec((tm, LANES), lambda i, j: (i, 0))],
        out_specs=(pl.BlockSpec((tm, tn), lambda i, j: (i, j)),
                   pl.BlockSpec((tm, tn), lambda i, j: (i, j))),
        compiler_params=_cparams(("arbitrary", "arbitrary")),
        name="proj_rope",
    )(h, w, cos2, sin2)


def _proj_act_kernel(h_ref, w_ref, o_ref, *, act):
    acc = jnp.dot(h_ref[...], w_ref[...], preferred_element_type=F32)
    s = _sigmoid(acc)
    o_ref[...] = (acc * s if act == "silu" else s).astype(o_ref.dtype)


def proj_act(h, w, act, tm, tn):
    m, k = h.shape
    n = w.shape[1]
    return pl.pallas_call(
        functools.partial(_proj_act_kernel, act=act),
        out_shape=jax.ShapeDtypeStruct((m, n), F32),
        grid=(m // tm, n // tn),
        in_specs=[pl.BlockSpec((tm, k), lambda i, j: (i, 0)),
                  pl.BlockSpec((k, tn), lambda i, j: (0, j))],
        out_specs=pl.BlockSpec((tm, tn), lambda i, j: (i, j)),
        compiler_params=_cparams(("arbitrary", "arbitrary")),
        name="proj_" + act,
    )(h, w)


def _compress_kernel(s_ref, pe_ref, w1_ref, w2_ref, o_ref):
    s = s_ref[0, 0]
    half = s.shape[1]
    top = jnp.dot((s + pe_ref[0, 0:1]).astype(BF16), w1_ref[0, :half], preferred_element_type=F32)
    bot = jnp.dot((s + pe_ref[0, 1:2]).astype(BF16), w1_ref[0, half:], preferred_element_type=F32)
    hid = top + pltpu.roll(bot, s.shape[0] - 1, axis=0)
    hid = (hid * _sigmoid(hid)).astype(BF16)
    o_ref[0, 0] = jnp.dot(hid, w2_ref[0], preferred_element_type=F32).astype(BF16)


def compress(sub, pe2, w1, w2):
    b, four, n_sub, width = sub.shape
    return pl.pallas_call(
        _compress_kernel,
        out_shape=jax.ShapeDtypeStruct((b, four, n_sub, HEAD_DIM), BF16),
        grid=(four, b),
        in_specs=[pl.BlockSpec((1, 1, n_sub, width), lambda j, i: (i, j, 0, 0)),
                  pl.BlockSpec((1, 2, width), lambda j, i: (j // 2, 0, 0)),
                  pl.BlockSpec((1, 2 * width, PHI_HIDDEN), lambda j, i: (j // 2, 0, 0)),
                  pl.BlockSpec((1, PHI_HIDDEN, HEAD_DIM), lambda j, i: (j // 2, 0, 0))],
        out_specs=pl.BlockSpec((1, 1, n_sub, HEAD_DIM), lambda j, i: (i, j, 0, 0)),
        compiler_params=_cparams(("arbitrary", "arbitrary")),
        name="compress",
    )(sub, pe2, w1, w2)


def _kmean_kernel(k_ref, o_ref):
    k = k_ref[0]
    nb = k.shape[0] // MOBA_BLOCK
    o_ref[0] = jnp.mean(k.reshape(nb, MOBA_BLOCK, k.shape[1]), axis=1)


def block_means(rows, col_block, width, blocks_per_step=8):
    b, t, _ = rows.shape
    nb = t // MOBA_BLOCK
    return pl.pallas_call(
        _kmean_kernel,
        out_shape=jax.ShapeDtypeStruct((b, nb, width), F32),
        grid=(b, nb // blocks_per_step),
        in_specs=[pl.BlockSpec((1, blocks_per_step * MOBA_BLOCK, width),
                               lambda i, j: (i, j, col_block))],
        out_specs=pl.BlockSpec((1, blocks_per_step, width), lambda i, j: (i, j, 0)),
        compiler_params=_cparams(("arbitrary", "arbitrary")),
        name="block_means",
    )(rows)


def _masked_softmax_cols(s, valid):
    s = jnp.where(valid, s, -jnp.inf)
    m = jnp.max(s, axis=0, keepdims=True)
    m = jnp.where(m > -jnp.inf, m, 0.0)
    p = jnp.exp(s - m)
    return p / jnp.maximum(jnp.sum(p, axis=0, keepdims=True), 1e-30)


def _transpose(x):
    rows, cols = x.shape
    pr, pc = -rows % LANES, -cols % LANES
    if pr:
        x = jnp.concatenate([x, jnp.zeros((pr, cols), x.dtype)], axis=0)
    if pc:
        x = jnp.concatenate([x, jnp.zeros((rows + pr, pc), x.dtype)], axis=1)
    return x.T[:cols, :rows]


def _select_top(score, blk, n_iter, limit):
    sel = jnp.zeros(score.shape, jnp.bool_)
    s = score
    blk = blk.astype(F32)
    n_blk = float(score.shape[0])
    for it in range(n_iter):
        mx = jnp.max(s, axis=0, keepdims=True)
        idx = jnp.min(jnp.where(s == mx, blk, n_blk), axis=0, keepdims=True)
        pick = blk == idx
        ok = mx > -jnp.inf
        if limit is not None:
            ok = ok & (limit > it)
        sel = sel | (pick & ok)
        s = jnp.where(pick, -jnp.inf, s)
    return sel


def _block_sparse_flash(qp, k_ref, v_ref, tq_lane, cs, tk, blk_shift, scale):
    r = qp.shape[0]
    row = lax.broadcasted_iota(jnp.int32, (tk, LANES), 0)
    lane = lax.broadcasted_iota(jnp.int32, (tk, LANES), 1)
    rel_blk = lane - (row >> blk_shift)
    blocks_per_tile = tk >> blk_shift

    def scores(i):
        start = pl.multiple_of(i * tk, tk)
        k_t = k_ref[0, pl.ds(start, tk), :]
        onehot = jnp.where(rel_blk == i * blocks_per_tile, 1.0, 0.0).astype(BF16)
        kp = jnp.concatenate([k_t, onehot], axis=1)
        return lax.dot_general(kp, qp, NT_DIMS, preferred_element_type=F32) * scale

    def update(carry, s, i):
        m, l, acc = carry
        m_new = jnp.maximum(m, jnp.max(s, axis=0, keepdims=True))
        alpha = jnp.exp(m - m_new)
        p = jnp.exp(s - m_new)
        l = alpha * l + jnp.sum(p, axis=0, keepdims=True)
        start = pl.multiple_of(i * tk, tk)
        v_t = v_ref[0, pl.ds(start, tk), :]
        pv = lax.dot_general(v_t, p.astype(BF16), TN_DIMS, preferred_element_type=F32)
        return m_new, l, alpha * acc + pv

    def body(i, carry):
        return update(carry, scores(i), i)

    init = (jnp.full((1, r), M_INIT, F32), jnp.zeros((1, r), F32), jnp.zeros((HEAD_DIM, r), F32))
    n_full = cs // tk
    carry = lax.fori_loop(0, n_full, body, init)
    kpos = n_full * tk + lax.broadcasted_iota(jnp.int32, (tk, 1), 0)
    s = jnp.where(kpos <= tq_lane, scores(n_full), M_INIT)
    _, l, acc = update(carry, s, n_full)
    return acc / l


def _nsa_kernel(q_ref, ga_ref, z_ref, kc_ref, vc_ref, ks_ref, vs_ref, kw_ref, vw_ref, pool_ref,
                o_ref, *, cq, tk, qpos0, n_cmp, win_base):
    c = pl.program_id(2)
    r = NSA_GROUP * cq
    cs = qpos0 + c * cq
    scale = HEAD_DIM ** -0.5
    q = q_ref[0]
    qs = jnp.concatenate([q[:, h * LANES:(h + 1) * LANES] for h in range(NSA_GROUP)], axis=0)
    tq_lane = cs + (lax.broadcasted_iota(jnp.int32, (1, r), 1) & (cq - 1))

    kc = kc_ref[0, 0]
    nc_pad = kc.shape[0]
    n_id = lax.broadcasted_iota(jnp.int32, (nc_pad, 1), 0)
    sc = lax.dot_general(kc, qs, NT_DIMS, preferred_element_type=F32) * scale
    valid = ((n_id * CMP_STRIDE + (CMP_BLOCK - 1)) <= tq_lane) & (n_id < n_cmp)
    pc = _masked_softmax_cols(sc, valid)
    o_cmp = lax.dot_general(vc_ref[0, 0], pc.astype(BF16), TN_DIMS, preferred_element_type=F32)

    psum = pc[:, 0:cq]
    for h in range(1, NSA_GROUP):
        psum = psum + pc[:, h * cq:(h + 1) * cq]
    hi = psum.astype(BF16)
    r1 = psum - hi.astype(F32)
    mid = r1.astype(BF16)
    lo = (r1 - mid.astype(F32)).astype(BF16)
    imp = jnp.dot(pool_ref[...], jnp.concatenate([hi, mid, lo], axis=0), preferred_element_type=F32)

    blk = lax.broadcasted_iota(jnp.int32, (LANES, 1), 0)
    tq = cs + lax.broadcasted_iota(jnp.int32, (1, cq), 1)
    own = tq >> 6
    open_blk = blk * SEL_BLOCK <= tq
    forced = open_blk & ((blk == 0) | (blk == own) | (blk == own - 1))
    score = jnp.where(forced, jnp.inf, jnp.where(open_blk, imp, -jnp.inf))
    n_beyond = (own >= LANES).astype(jnp.int32) + (own - 1 >= LANES).astype(jnp.int32)
    sel = _select_top(score, blk, N_SEL, N_SEL - n_beyond)
    bias = _transpose(jnp.where(sel, 0.0, MASK_BIAS)).astype(BF16)
    qp = jnp.concatenate([qs, jnp.concatenate([bias] * NSA_GROUP, axis=0)], axis=1)
    o_slc = _block_sparse_flash(qp, ks_ref, vs_ref, tq_lane, cs, tk, 6, scale)

    tw = WINDOW + cq
    w_start = pl.multiple_of(c * cq, cq)
    k_w = kw_ref[0, pl.ds(w_start, tw), :]
    v_w = vw_ref[0, pl.ds(w_start, tw), :]
    wpos = (cs - WINDOW) + lax.broadcasted_iota(jnp.int32, (tw, 1), 0)
    sw = lax.dot_general(k_w, qs, NT_DIMS, preferred_element_type=F32) * scale
    wvalid = (wpos <= tq_lane) & (wpos > tq_lane - WINDOW) & (wpos >= win_base)
    pw = _masked_softmax_cols(sw, wvalid)
    o_win = lax.dot_general(v_w, pw.astype(BF16), TN_DIMS, preferred_element_type=F32)

    g = _transpose(ga_ref[0])
    for h in range(NSA_GROUP):
        cols = slice(h * cq, (h + 1) * cq)
        o = (g[3 * h:3 * h + 1] * o_cmp[:, cols] + g[3 * h + 1:3 * h + 2] * o_slc[:, cols]
             + g[3 * h + 2:3 * h + 3] * o_win[:, cols])
        o_ref[0, :, h * LANES:(h + 1) * LANES] = (
            _transpose(o) * z_ref[0, :, h * LANES:(h + 1) * LANES]).astype(BF16)


def nsa_attention(q, q_col, gates, z, z_col, kc, ks, ks_col, vs_col, kw, kw_col, vw_col, pool,
                  *, cq, tk, qpos0, n_cmp, win_base):
    b, tq_len, _ = q.shape
    tkv = ks.shape[1]
    tw_len = kw.shape[1]
    nc_pad = kc.shape[2]
    gw = NSA_GROUP * LANES
    kern = functools.partial(_nsa_kernel, cq=cq, tk=tk, qpos0=qpos0, n_cmp=n_cmp, win_base=win_base)
    return pl.pallas_call(
        kern,
        out_shape=jax.ShapeDtypeStruct((b, tq_len, NSA_HEADS * HEAD_DIM), BF16),
        grid=(b, NSA_KV_HEADS, tq_len // cq),
        in_specs=[
            pl.BlockSpec((1, cq, gw), lambda i, g, c: (i, c, q_col + g)),
            pl.BlockSpec((1, cq, LANES), lambda i, g, c: (i, c, g)),
            pl.BlockSpec((1, cq, gw), lambda i, g, c: (i, c, z_col + g)),
            pl.BlockSpec((1, 1, nc_pad, HEAD_DIM), lambda i, g, c: (i, g, 0, 0)),
            pl.BlockSpec((1, 1, nc_pad, HEAD_DIM), lambda i, g, c: (i, 2 + g, 0, 0)),
            pl.BlockSpec((1, tkv, HEAD_DIM), lambda i, g, c: (i, 0, ks_col + g)),
            pl.BlockSpec((1, tkv, HEAD_DIM), lambda i, g, c: (i, 0, vs_col + g)),
            pl.BlockSpec((1, tw_len, HEAD_DIM), lambda i, g, c: (i, 0, kw_col + g)),
            pl.BlockSpec((1, tw_len, HEAD_DIM), lambda i, g, c: (i, 0, vw_col + g)),
            pl.BlockSpec(pool.shape, lambda i, g, c: (0, 0)),
        ],
        out_specs=pl.BlockSpec((1, cq, gw), lambda i, g, c: (i, c, g)),
        compiler_params=_cparams(("arbitrary", "arbitrary", "arbitrary")),
        name="nsa_attention",
    )(q, gates, z, kc, kc, ks, ks, kw, kw, pool)


def _moba_kernel(q_ref, z_ref, km_ref, k_ref, v_ref, o_ref, *, cq, tk, qpos0):
    c = pl.program_id(2)
    cs = qpos0 + c * cq
    scale = HEAD_DIM ** -0.5
    q = q_ref[0]
    tq = cs + lax.broadcasted_iota(jnp.int32, (1, cq), 1)
    blk = lax.broadcasted_iota(jnp.int32, (LANES, 1), 0)
    own = tq >> 8
    sc = lax.dot_general(km_ref[0], q, NT_DIMS, preferred_element_type=F32)
    score = jnp.where(blk < own, sc, -jnp.inf)
    sel = _select_top(score, blk, MOBA_TOPK, None) | (blk == own)
    bias = _transpose(jnp.where(sel, 0.0, MASK_BIAS)).astype(BF16)
    qp = jnp.concatenate([q, bias], axis=1)
    o = _block_sparse_flash(qp, k_ref, v_ref, tq, cs, tk, 8, scale)
    o_ref[0] = (_transpose(o) * z_ref[0]).astype(BF16)


def moba_attention(q, q_col, z, z_col, kmean, k, k_col, v_col, *, cq, tk, qpos0):
    b, tq_len, _ = q.shape
    tkv = k.shape[1]
    kern = functools.partial(_moba_kernel, cq=cq, tk=tk, qpos0=qpos0)
    return pl.pallas_call(
        kern,
        out_shape=jax.ShapeDtypeStruct((b, tq_len, MOBA_HEADS * HEAD_DIM), BF16),
        grid=(b, MOBA_HEADS, tq_len // cq),
        in_specs=[
            pl.BlockSpec((1, cq, LANES), lambda i, h, c: (i, c, q_col + h)),
            pl.BlockSpec((1, cq, LANES), lambda i, h, c: (i, c, z_col + h)),
            pl.BlockSpec((1, LANES, HEAD_DIM), lambda i, h, c: (i, 0, h)),
            pl.BlockSpec((1, tkv, HEAD_DIM), lambda i, h, c: (i, 0, k_col + h)),
            pl.BlockSpec((1, tkv, HEAD_DIM), lambda i, h, c: (i, 0, v_col + h)),
        ],
        out_specs=pl.BlockSpec((1, cq, LANES), lambda i, h, c: (i, c, h)),
        compiler_params=_cparams(("arbitrary", "arbitrary", "arbitrary")),
        name="moba_attention",
    )(q, z, kmean, k, k)


def _merge_kernel(oa_ref, ob_ref, ga_ref, gb_ref, wa_ref, wb_ref, o_ref):
    br_a = jnp.dot(oa_ref[...], wa_ref[...], preferred_element_type=F32)
    br_b = jnp.dot(ob_ref[...], wb_ref[...], preferred_element_type=F32)
    o_ref[...] = (ga_ref[...] * br_a + gb_ref[...] * br_b).astype(BF16)


def merge_branches(oz_a, oz_b, g_m, w_a, w_b, tm):
    m, ka = oz_a.shape
    d = w_a.shape[1]
    return pl.pallas_call(
        _merge_kernel,
        out_shape=jax.ShapeDtypeStruct((m, d), BF16),
        grid=(m // tm,),
        in_specs=[pl.BlockSpec((tm, ka), lambda i: (i, 0)),
                  pl.BlockSpec((tm, ka), lambda i: (i, 0)),
                  pl.BlockSpec((tm, d), lambda i: (i, 0)),
                  pl.BlockSpec((tm, d), lambda i: (i, 1)),
                  pl.BlockSpec((ka, d), lambda i: (0, 0)),
                  pl.BlockSpec((ka, d), lambda i: (0, 0))],
        out_specs=pl.BlockSpec((tm, d), lambda i: (i, 0)),
        compiler_params=_cparams(("arbitrary",)),
        name="merge_branches",
    )(oz_a, oz_b, g_m, g_m, w_a, w_b)


def _final_kernel(m_ref, x_ref, gate_ref, w_ref, g_ref, o_ref):
    y = x_ref[...] + gate_ref[...] * jnp.dot(m_ref[...], w_ref[...], preferred_element_type=F32)
    ms = jnp.mean(y * y, axis=-1, keepdims=True)
    o_ref[...] = y * lax.rsqrt(ms + NORM_EPS) * g_ref[...]


def final_out(merged, x, gate, w, g, tm):
    m, d = x.shape
    per_row = gate.shape[0] != 1
    gate_spec = (pl.BlockSpec((tm, d), lambda i: (i, 0)) if per_row
                 else pl.BlockSpec((1, d), lambda i: (0, 0)))
    return pl.pallas_call(
        _final_kernel,
        out_shape=jax.ShapeDtypeStruct((m, d), F32),
        grid=(m // tm,),
        in_specs=[pl.BlockSpec((tm, d), lambda i: (i, 0)),
                  pl.BlockSpec((tm, d), lambda i: (i, 0)),
                  gate_spec,
                  pl.BlockSpec((d, d), lambda i: (0, 0)),
                  pl.BlockSpec((1, d), lambda i: (0, 0))],
        out_specs=pl.BlockSpec((tm, d), lambda i: (i, 0)),
        compiler_params=_cparams(("arbitrary",)),
        name="final_out",
    )(merged, x, gate, w, g)


HEAD_ROWS = 16


def _softmax_rows_with_extra(s, valid, s_new):
    s = jnp.where(valid, s, -jnp.inf)
    m = jnp.maximum(jnp.max(s, axis=1, keepdims=True), s_new)
    p = jnp.exp(s - m)
    p_new = jnp.exp(s_new - m)
    l = jnp.sum(p, axis=1, keepdims=True) + p_new
    return p / l, p_new / l


def _bf16_round(x):
    return x.astype(BF16).astype(F32)


def _decode_cmp_kernel(pt_ref, cache_ref, q_ref, pe_ref, w1_ref, w2_ref, pool_ref,
                       ocmp_ref, imp_ref, bufk, bufv, xtop, xbot, sem,
                       *, n_pages, page, n_cmp, qpos):
    s = pl.program_id(0)
    n_seq = pl.num_programs(0)
    n_sub = n_pages * page // CMP_STRIDE
    half = CMP_STRIDE * HEAD_DIM
    scale = HEAD_DIM ** -0.5

    def page_copies(seq, kv, buf, fn):
        def body(p, c):
            pid = pt_ref[seq, p]
            for g in range(NSA_KV_HEADS):
                fn(pltpu.make_async_copy(cache_ref.at[pid, :, kv * NSA_KV_HEADS + g, :],
                                         buf.at[g, pl.ds(p * page, page), :], sem.at[kv]))
            return c
        lax.fori_loop(0, n_pages, body, 0)

    def start(cp):
        cp.start()

    def wait(cp):
        cp.wait()

    def compress_half(buf, kv):
        for r in range(CMP_STRIDE):
            x = jnp.concatenate([buf[g, pl.ds(r, n_sub, stride=CMP_STRIDE), :]
                                 for g in range(NSA_KV_HEADS)], axis=0)
            cols = slice(r * HEAD_DIM, (r + 1) * HEAD_DIM)
            xtop[:, cols] = (x + pe_ref[kv, 0:1, cols]).astype(BF16)
            xbot[:, cols] = (x + pe_ref[kv, 1:2, cols]).astype(BF16)
        top = jnp.dot(xtop[...], w1_ref[kv, :half], preferred_element_type=F32)
        bot = jnp.dot(xbot[...], w1_ref[kv, half:], preferred_element_type=F32)
        outs = []
        for g in range(NSA_KV_HEADS):
            rows = slice(g * n_sub, (g + 1) * n_sub)
            hid = top[rows] + pltpu.roll(bot[rows], n_sub - 1, axis=0)
            hid = (hid * _sigmoid(hid)).astype(BF16)
            outs.append(jnp.dot(hid, w2_ref[kv], preferred_element_type=F32).astype(BF16))
        return outs

    @pl.when(s == 0)
    def _():
        page_copies(0, 0, bufk, start)
        page_copies(0, 1, bufv, start)

    page_copies(s, 0, bufk, wait)
    kc = compress_half(bufk, 0)

    @pl.when(s + 1 < n_seq)
    def _():
        page_copies(s + 1, 0, bufk, start)

    page_copies(s, 1, bufv, wait)
    vc = compress_half(bufv, 1)

    @pl.when(s + 1 < n_seq)
    def _():
        page_copies(s + 1, 1, bufv, start)

    q = q_ref[0].astype(BF16)
    row_g = lax.broadcasted_iota(jnp.int32, (HEAD_ROWS, 1), 0) >> 2
    sc = [lax.dot_general(q, kc[g], NT_DIMS, preferred_element_type=F32) for g in range(NSA_KV_HEADS)]
    sc = jnp.where(row_g == 0, sc[0], sc[1]) * scale
    n_id = lax.broadcasted_iota(jnp.int32, (1, n_sub), 1)
    valid = ((n_id * CMP_STRIDE + (CMP_BLOCK - 1)) <= qpos) & (n_id < n_cmp)
    sc = jnp.where(valid, sc, -jnp.inf)
    m = jnp.max(sc, axis=1, keepdims=True)
    m = jnp.where(m > -jnp.inf, m, 0.0)
    p = jnp.exp(sc - m)
    pc = p / jnp.maximum(jnp.sum(p, axis=1, keepdims=True), 1e-30)
    pcb = pc.astype(BF16)
    o = [jnp.dot(pcb, vc[g], preferred_element_type=F32) for g in range(NSA_KV_HEADS)]
    ocmp_ref[0] = jnp.where(row_g == 0, o[0], o[1])
    psum = jnp.concatenate(
        [jnp.sum(pc[g * NSA_GROUP:(g + 1) * NSA_GROUP], axis=0, keepdims=True) for g in range(NSA_KV_HEADS)]
        + [jnp.zeros((HEAD_ROWS - NSA_KV_HEADS, n_sub), F32)], axis=0)
    hi = psum.astype(BF16)
    r1 = psum - hi.astype(F32)
    mid = r1.astype(BF16)
    lo = (r1 - mid.astype(F32)).astype(BF16)
    imp_ref[0] = jnp.dot(jnp.concatenate([hi, mid, lo], axis=1), pool_ref[...], preferred_element_type=F32)


def decode_cmp(cache_view, page_table, q16, pe2, w1, w2, pool_nat, *, n_cmp, qpos):
    bs, n_pages = page_table.shape
    page = cache_view.shape[1]
    past = n_pages * page
    n_sub = past // CMP_STRIDE
    nb = pool_nat.shape[1]
    half = CMP_STRIDE * HEAD_DIM

    def whole(a):
        return pl.BlockSpec(a.shape, lambda i, pt: (0,) * a.ndim)

    grid_spec = pltpu.PrefetchScalarGridSpec(
        num_scalar_prefetch=1,
        grid=(bs,),
        in_specs=[pl.BlockSpec(memory_space=pl.ANY),
                  pl.BlockSpec((1, HEAD_ROWS, HEAD_DIM), lambda i, pt: (i, 0, 0)),
                  whole(pe2), whole(w1), whole(w2), whole(pool_nat)],
        out_specs=(pl.BlockSpec((1, HEAD_ROWS, HEAD_DIM), lambda i, pt: (i, 0, 0)),
                   pl.BlockSpec((1, HEAD_ROWS, nb), lambda i, pt: (i, 0, 0))),
        scratch_shapes=[pltpu.VMEM((NSA_KV_HEADS, past, HEAD_DIM), F32),
                        pltpu.VMEM((NSA_KV_HEADS, past, HEAD_DIM), F32),
                        pltpu.VMEM((NSA_KV_HEADS * n_sub, half), BF16),
                        pltpu.VMEM((NSA_KV_HEADS * n_sub, half), BF16),
                        pltpu.SemaphoreType.DMA((2,))],
    )
    return pl.pallas_call(
        functools.partial(_decode_cmp_kernel, n_pages=n_pages, page=page, n_cmp=n_cmp, qpos=qpos),
        out_shape=(jax.ShapeDtypeStruct((bs, HEAD_ROWS, HEAD_DIM), F32),
                   jax.ShapeDtypeStruct((bs, HEAD_ROWS, nb), F32)),
        grid_spec=grid_spec,
        compiler_params=_cparams(("arbitrary",)),
        name="decode_cmp",
    )(page_table, cache_view, q16, pe2, w1, w2, pool_nat)


def _decode_kmean_kernel(pt_ref, *refs, n_in, pages_per_block):
    o_ref = refs[n_in]
    for b in range(n_in // pages_per_block):
        tot = jnp.sum(refs[b * pages_per_block][0], axis=0)
        for pp in range(1, pages_per_block):
            tot = tot + jnp.sum(refs[b * pages_per_block + pp][0], axis=0)
        o_ref[0, b] = tot / MOBA_BLOCK


def decode_kmean(cache_view, page_table, blocks_per_step):
    bs, n_pages = page_table.shape
    page = cache_view.shape[1]
    ppb = MOBA_BLOCK // page
    n_blocks = n_pages // ppb
    n_in = blocks_per_step * ppb
    in_specs = [pl.BlockSpec((1, page, MOBA_HEADS, HEAD_DIM),
                             lambda i, j, pt, k=k: (pt[i, j * n_in + k], 0, 0, 0)) for k in range(n_in)]
    grid_spec = pltpu.PrefetchScalarGridSpec(
        num_scalar_prefetch=1,
        grid=(bs, n_blocks // blocks_per_step),
        in_specs=in_specs,
        out_specs=pl.BlockSpec((1, blocks_per_step, MOBA_HEADS, HEAD_DIM), lambda i, j, pt: (i, j, 0, 0)),
    )
    return pl.pallas_call(
        functools.partial(_decode_kmean_kernel, n_in=n_in, pages_per_block=ppb),
        out_shape=jax.ShapeDtypeStruct((bs, n_blocks, MOBA_HEADS, HEAD_DIM), F32),
        grid_spec=grid_spec,
        compiler_params=_cparams(("arbitrary", "arbitrary")),
        name="decode_kmean",
    )(page_table, *([cache_view] * n_in))


def _top_indices(score, blk, n_iter):
    rows = []
    s = score
    blk = blk.astype(F32)
    n_blk = float(score.shape[0])
    for _ in range(n_iter):
        mx = jnp.max(s, axis=0, keepdims=True)
        idx = jnp.min(jnp.where(s == mx, blk, n_blk), axis=0, keepdims=True)
        rows.append(jnp.where(mx > -jnp.inf, idx, -1.0))
        s = jnp.where(blk == idx, -jnp.inf, s)
    return rows


def _decode_select_kernel(imp_ref, km_ref, q_ref, nidx_ref, midx_ref, *, qpos):
    nbk = imp_ref.shape[0]
    blk = lax.broadcasted_iota(jnp.int32, (nbk, 1), 0)
    own = qpos // SEL_BLOCK
    open_blk = blk * SEL_BLOCK <= qpos
    forced = open_blk & ((blk == 0) | (blk == own) | (blk == own - 1))
    score = jnp.where(forced, jnp.inf, jnp.where(open_blk, imp_ref[...], -jnp.inf))
    nidx_ref[...] = jnp.concatenate(_top_indices(score, blk, N_SEL), axis=0).astype(jnp.int32)

    n_seq, n_blocks = km_ref.shape[0], km_ref.shape[1]
    n_col = n_seq * MOBA_HEADS
    lane = lax.broadcasted_iota(jnp.int32, (1, n_col), 1)

    def body(i, sct):
        q = _bf16_round(q_ref[i])
        for h in range(MOBA_HEADS):
            prod = _bf16_round(km_ref[i, :, h, :]) * q[h:h + 1]
            col = jnp.sum(prod, axis=1, keepdims=True)
            sct = jnp.where(lane == i * MOBA_HEADS + h, col, sct)
        return sct

    sct = lax.fori_loop(0, n_seq, body, jnp.zeros((n_blocks, n_col), F32))
    blk_m = lax.broadcasted_iota(jnp.int32, (n_blocks, 1), 0)
    score_m = jnp.where(blk_m < qpos // MOBA_BLOCK, sct, -jnp.inf)
    rows = _top_indices(score_m, blk_m, MOBA_TOPK)
    rows = rows + [jnp.full((1, n_col), -1.0, F32)] * (midx_ref.shape[0] - MOBA_TOPK)
    midx_ref[...] = jnp.concatenate(rows, axis=0).astype(jnp.int32)


def decode_select(imp_t, kmean, q16, *, qpos):
    n_col_a = imp_t.shape[1]
    n_col_b = kmean.shape[0] * MOBA_HEADS

    def whole(a):
        return pl.BlockSpec(a.shape, lambda i: (0,) * a.ndim)

    return pl.pallas_call(
        functools.partial(_decode_select_kernel, qpos=qpos),
        out_shape=(jax.ShapeDtypeStruct((N_SEL, n_col_a), jnp.int32),
                   jax.ShapeDtypeStruct((8, n_col_b), jnp.int32)),
        grid=(1,),
        in_specs=[whole(imp_t), whole(kmean), whole(q16)],
        out_specs=(pl.BlockSpec((N_SEL, n_col_a), lambda i: (0, 0)),
                   pl.BlockSpec((8, n_col_b), lambda i: (0, 0))),
        compiler_params=_cparams(("arbitrary",)),
        name="decode_select",
    )(imp_t, kmean, q16)


def _decode_attend_kernel(pt_ref, nidx_ref, midx_ref, nsa_ref, moba_ref,
                          qa_ref, qb_ref, ga_ref, za_ref, zb_ref, ocmp_ref, newa_ref, newb_ref, win_ref,
                          oa_ref, ob_ref, ks, vs, km, vm, sem, *, page, past, win_base):
    s = pl.program_id(0)
    n_seq = pl.num_programs(0)
    slot = lax.rem(s, 2)
    scale = HEAD_DIM ** -0.5
    blocks_per_page = page // SEL_BLOCK
    pages_per_block = MOBA_BLOCK // page
    n_cache_blk = past // SEL_BLOCK

    def nsa_copies(seq, sl, fn):
        for g in range(NSA_KV_HEADS):
            def body(i, c, g=g):
                blk = jnp.clip(nidx_ref[i, seq * NSA_KV_HEADS + g], 0, n_cache_blk - 1)
                pid = pt_ref[seq, blk // blocks_per_page]
                rows = pl.ds((blk % blocks_per_page) * SEL_BLOCK, SEL_BLOCK)
                dst = pl.ds(i * SEL_BLOCK, SEL_BLOCK)
                fn(pltpu.make_async_copy(nsa_ref.at[pid, rows, 2 * NSA_KV_HEADS + g, :],
                                         ks.at[sl, g, dst, :], sem.at[sl, 0]))
                fn(pltpu.make_async_copy(nsa_ref.at[pid, rows, 3 * NSA_KV_HEADS + g, :],
                                         vs.at[sl, g, dst, :], sem.at[sl, 0]))
                return c
            lax.fori_loop(0, N_SEL, body, 0)

    def moba_copies(seq, sl, fn):
        for h in range(MOBA_HEADS):
            def body(i, c, h=h):
                blk = jnp.maximum(midx_ref[i, seq * MOBA_HEADS + h], 0)
                for pp in range(pages_per_block):
                    pid = pt_ref[seq, blk * pages_per_block + pp]
                    dst = pl.ds((i * pages_per_block + pp) * page, page)
                    fn(pltpu.make_async_copy(moba_ref.at[pid, :, h, :], km.at[sl, h, dst, :], sem.at[sl, 1]))
                    fn(pltpu.make_async_copy(moba_ref.at[pid, :, MOBA_HEADS + h, :],
                                             vm.at[sl, h, dst, :], sem.at[sl, 1]))
                return c
            lax.fori_loop(0, MOBA_TOPK, body, 0)

    def start(cp):
        cp.start()

    def wait(cp):
        cp.wait()

    @pl.when(s == 0)
    def _():
        nsa_copies(0, 0, start)
        moba_copies(0, 0, start)

    @pl.when(s + 1 < n_seq)
    def _():
        nsa_copies(s + 1, 1 - slot, start)
        moba_copies(s + 1, 1 - slot, start)

    nsa_copies(s, slot, wait)
    moba_copies(s, slot, wait)

    row = lax.broadcasted_iota(jnp.int32, (HEAD_ROWS, 1), 0)
    row_g = row >> 2
    qa = qa_ref[0].astype(BF16)
    qa_f = qa.astype(F32)
    new_a = _bf16_round(newa_ref[0])

    n_keys = N_SEL * SEL_BLOCK
    lane_blk = lax.broadcasted_iota(jnp.int32, (1, n_keys), 1) >> (SEL_BLOCK.bit_length() - 1)
    o_slc = []
    for g in range(NSA_KV_HEADS):
        k = ks[slot, g].astype(BF16)
        v = vs[slot, g].astype(BF16)
        sc = lax.dot_general(qa, k, NT_DIMS, preferred_element_type=F32) * scale
        okv = jnp.zeros((1, n_keys), jnp.int32)
        for i in range(N_SEL):
            b = nidx_ref[i, s * NSA_KV_HEADS + g]
            ok = ((b >= 0) & (b < n_cache_blk)).astype(jnp.int32)
            okv = jnp.where(lane_blk == i, ok, okv)
        s_new = jnp.sum(qa_f * new_a[g:g + 1], axis=1, keepdims=True) * scale
        pn, pn_new = _softmax_rows_with_extra(sc, okv > 0, s_new)
        o_slc.append(jnp.dot(pn.astype(BF16), v, preferred_element_type=F32)
                     + _bf16_round(pn_new) * new_a[NSA_KV_HEADS + g:NSA_KV_HEADS + g + 1])
    o_slc = jnp.where(row_g == 0, o_slc[0], o_slc[1])

    half_w = win_ref.shape[1]
    pos_e = win_base + 2 * lax.broadcasted_iota(jnp.int32, (1, half_w), 1)
    o_win = []
    for g in range(NSA_KV_HEADS):
        parts = []
        for parity in range(2):
            k = win_ref[0, :, 4 * parity + g, :].astype(BF16)
            sc = lax.dot_general(qa, k, NT_DIMS, preferred_element_type=F32) * scale
            pos = pos_e + parity
            ok = (pos <= past) & (pos > past - WINDOW) & (pos >= win_base)
            parts.append(jnp.where(ok, sc, -jnp.inf))
        s_new = jnp.sum(qa_f * new_a[4 + g:5 + g], axis=1, keepdims=True) * scale
        sc = jnp.concatenate(parts, axis=1)
        pn, pn_new = _softmax_rows_with_extra(sc, sc > -jnp.inf, s_new)
        o = _bf16_round(pn_new) * new_a[6 + g:7 + g]
        for parity in range(2):
            v = win_ref[0, :, 4 * parity + 2 + g, :].astype(BF16)
            o = o + jnp.dot(pn[:, parity * half_w:(parity + 1) * half_w].astype(BF16), v,
                            preferred_element_type=F32)
        o_win.append(o)
    o_win = jnp.where(row_g == 0, o_win[0], o_win[1])

    gates = ga_ref[0]
    o_a = gates[:, 0:1] * ocmp_ref[0] + gates[:, 1:2] * o_slc + gates[:, 2:3] * o_win
    oa_ref[0] = (o_a * za_ref[0]).astype(BF16)

    qb = qb_ref[0].astype(BF16)
    qb_f = qb.astype(F32)
    new_b = _bf16_round(newb_ref[0])
    n_keys_b = MOBA_TOPK * MOBA_BLOCK
    lane_blk_b = lax.broadcasted_iota(jnp.int32, (1, n_keys_b), 1) >> (MOBA_BLOCK.bit_length() - 1)
    o_b = jnp.zeros((HEAD_ROWS, HEAD_DIM), F32)
    for h in range(MOBA_HEADS):
        k = km[slot, h].astype(BF16)
        v = vm[slot, h].astype(BF16)
        sc = lax.dot_general(qb, k, NT_DIMS, preferred_element_type=F32) * scale
        okv = jnp.zeros((1, n_keys_b), jnp.int32)
        for i in range(MOBA_TOPK):
            ok = (midx_ref[i, s * MOBA_HEADS + h] >= 0).astype(jnp.int32)
            okv = jnp.where(lane_blk_b == i, ok, okv)
        s_new = jnp.sum(qb_f * new_b[h:h + 1], axis=1, keepdims=True) * scale
        pn, pn_new = _softmax_rows_with_extra(sc, okv > 0, s_new)
        o = (jnp.dot(pn.astype(BF16), v, preferred_element_type=F32)
             + _bf16_round(pn_new) * new_b[MOBA_HEADS + h:MOBA_HEADS + h + 1])
        o_b = jnp.where(row == h, o, o_b)
    ob_ref[0] = (o_b * zb_ref[0]).astype(BF16)


def decode_attend(nsa_view, moba_view, page_table, nidx, midx, per_seq, win_view, *, past, win_base):
    bs = page_table.shape[0]
    page = nsa_view.shape[1]
    row_spec = pl.BlockSpec((1, HEAD_ROWS, HEAD_DIM), lambda i, *_: (i, 0, 0))
    grid_spec = pltpu.PrefetchScalarGridSpec(
        num_scalar_prefetch=3,
        grid=(bs,),
        in_specs=[pl.BlockSpec(memory_space=pl.ANY), pl.BlockSpec(memory_space=pl.ANY)]
        + [row_spec] * len(per_seq)
        + [pl.BlockSpec((1,) + win_view.shape[1:], lambda i, *_: (i, 0, 0, 0))],
        out_specs=(row_spec, row_spec),
        scratch_shapes=[pltpu.VMEM((2, NSA_KV_HEADS, N_SEL * SEL_BLOCK, HEAD_DIM), F32),
                        pltpu.VMEM((2, NSA_KV_HEADS, N_SEL * SEL_BLOCK, HEAD_DIM), F32),
                        pltpu.VMEM((2, MOBA_HEADS, MOBA_TOPK * MOBA_BLOCK, HEAD_DIM), F32),
                        pltpu.VMEM((2, MOBA_HEADS, MOBA_TOPK * MOBA_BLOCK, HEAD_DIM), F32),
                        pltpu.SemaphoreType.DMA((2, 2))],
    )
    return pl.pallas_call(
        functools.partial(_decode_attend_kernel, page=page, past=past, win_base=win_base),
        out_shape=(jax.ShapeDtypeStruct((bs, HEAD_ROWS, HEAD_DIM), BF16),
                   jax.ShapeDtypeStruct((bs, HEAD_ROWS, HEAD_DIM), BF16)),
        grid_spec=grid_spec,
        compiler_params=_cparams(("arbitrary",)),
        name="decode_attend",
    )(page_table, nidx, midx, nsa_view, moba_view, *per_seq, win_view)


def _rope_tables(pos):
    half = HEAD_DIM // 2
    inv = ROPE_THETA ** (-(jnp.arange(half, dtype=F32) / half))
    ang = pos.astype(F32)[:, None] * inv[None, :]
    cos, sin = jnp.cos(ang), jnp.sin(ang)
    return jnp.concatenate([cos, cos], axis=1), jnp.concatenate([-sin, sin], axis=1)


def _pool_matrix(nc_pad, n_cmp, n_blk=LANES):
    ratio = SEL_BLOCK // CMP_STRIDE
    n_sub = CMP_BLOCK // CMP_STRIDE
    j = np.arange(n_blk)[:, None]
    n = np.arange(nc_pad)[None, :]
    pool = ((n >= ratio * j - (n_sub - 1)) & (n <= ratio * j + ratio - 1) & (n < n_cmp)).astype(np.float32)
    return jnp.asarray(np.concatenate([pool, pool, pool], axis=1), dtype=BF16)


def _pad_rows(a, rows):
    return jnp.pad(a, ((0, 0), (0, rows - a.shape[1]), (0, 0)))


def kernel(x_prompt, x_sample, cache_nsa, cache_moba, state_nsa_win, page_table, c_prompt, c_sample,
           w_ada, b_ada, norm_g, w_in, pe_k, w_phi_k1, w_phi_k2, pe_v, w_phi_v1, w_phi_v2,
           w_out_a, w_out_b, w_out, final_g):
    depth = w_in.shape[0]
    assert depth == 1, "single-layer step"
    d_model = x_prompt.shape[-1]
    bp, t_p, _ = x_prompt.shape
    bs, t_s, _ = x_sample.shape
    assert bp == 1 and t_s == 1
    page = cache_nsa.shape[2]
    n_pages = page_table.shape[1]
    past = n_pages * page
    wb = state_nsa_win.shape[2]
    w_a = NSA_HEADS * HEAD_DIM
    w_b = MOBA_HEADS * HEAD_DIM
    kv_a = NSA_KV_HEADS * HEAD_DIM
    s0 = w_a + 6 * kv_a
    s1 = s0 + 3 * NSA_HEADS
    s2 = s1 + w_a
    s3 = s2 + 3 * w_b
    s4 = s3 + w_b

    wi = w_in[0]
    w_nsa = wi[:, :s0].astype(BF16)
    w_moba = wi[:, s2:s3].astype(BF16)
    w_z = jnp.concatenate([wi[:, s1:s2], wi[:, s3:s4]], axis=1).astype(BF16)
    w_gm = wi[:, s4:].astype(BF16)
    w_ga = wi[:, s0:s1].reshape(d_model, NSA_KV_HEADS, 3 * NSA_GROUP)
    w_ga = jnp.pad(w_ga, ((0, 0), (0, 0), (0, LANES - 3 * NSA_GROUP))).reshape(d_model, -1).astype(BF16)
    w1 = jnp.stack([w_phi_k1[0], w_phi_v1[0]]).astype(BF16)
    w2 = jnp.stack([w_phi_k2[0], w_phi_v2[0]]).astype(BF16)
    half_w = CMP_STRIDE * HEAD_DIM
    pe2 = jnp.stack([pe_k[0].reshape(2, half_w), pe_v[0].reshape(2, half_w)])
    woa = w_out_a[0].astype(BF16)
    wob = w_out_b[0].astype(BF16)
    wo = w_out[0].astype(BF16)
    fg = final_g.reshape(1, d_model)
    ng = norm_g[0].reshape(1, d_model)

    c_all = jnp.concatenate([c_prompt, c_sample], axis=0)
    n_c = c_all.shape[0]
    c_all = jnp.pad(c_all, ((0, -n_c % 8), (0, 0)))
    mod = ada_mod(c_all, w_ada[0], b_ada[0].reshape(1, -1))
    shift, scale, gate = mod[:, :d_model], mod[:, d_model:2 * d_model], mod[:, 2 * d_model:]

    def project(x2d, rows, pos, tm):
        sl = slice(rows.start, rows.stop)
        h = norm_modulate(x2d, ng, scale[sl], shift[sl], tm)
        cos2, sin2 = _rope_tables(pos)
        nsa_f, nsa_b = proj_rope(h, w_nsa, cos2, sin2, tm, 512, w_a // 512, 2)
        moba_f, moba_b = proj_rope(h, w_moba, cos2, sin2, tm, 512, 2 * w_b // 512, 0)
        z = proj_act(h, w_z, "silu", tm, 512)
        g_m = proj_act(h, w_gm, "sigmoid", tm, 512)
        g_a = proj_act(h, w_ga, "sigmoid", tm, NSA_KV_HEADS * LANES)
        return nsa_f, nsa_b, moba_f, moba_b, z, g_m, g_a

    def finish(oz_a, oz_b, g_m, x2d, rows, tm):
        merged = merge_branches(oz_a, oz_b, g_m, woa, wob, tm)
        return final_out(merged, x2d, gate[rows.start:rows.stop], wo, fg, tm)

    assert t_p % MOBA_BLOCK == 0
    xp = x_prompt.reshape(t_p, d_model)
    tm_p = 1024 if t_p % 1024 == 0 else 256
    nsa_f, nsa_b, moba_f, moba_b, z_p, gm_p, ga_p = project(
        xp, slice(0, 1), jnp.arange(t_p, dtype=jnp.int32), tm_p)
    n_cmp_p = (t_p - CMP_BLOCK) // CMP_STRIDE + 1
    n_sub_p = t_p // CMP_STRIDE
    assert n_cmp_p == n_sub_p - 1 and n_sub_p % 8 == 0
    sub_p = nsa_f[:, w_a:w_a + 2 * kv_a].reshape(n_sub_p, CMP_STRIDE, 4, HEAD_DIM)
    sub_p = sub_p.transpose(2, 0, 1, 3).reshape(1, 4, n_sub_p, half_w)
    kc_p = compress(sub_p, pe2, w1, w2)
    pool_p = _pool_matrix(n_sub_p, n_cmp_p)
    cq_p = 128
    tk_p = 512 if t_p % 512 == 0 else 256
    assert (t_p - 1) // SEL_BLOCK < LANES
    win_p = jnp.pad(nsa_b[:, w_a + 4 * kv_a:], ((WINDOW, 0), (0, 0)))[None]
    nsa_b3 = nsa_b[None]
    oz_a = nsa_attention(
        nsa_b3, 0, ga_p[None], z_p[None], 0, kc_p,
        nsa_b3, (w_a + 2 * kv_a) // LANES, (w_a + 3 * kv_a) // LANES,
        win_p, 0, NSA_KV_HEADS, pool_p,
        cq=cq_p, tk=tk_p, qpos0=0, n_cmp=n_cmp_p, win_base=0)
    moba_f3 = moba_f[None]
    moba_b3 = moba_b[None]
    km_p = block_means(moba_f3, 1, w_b, blocks_per_step=8 if (t_p // MOBA_BLOCK) % 8 == 0 else t_p // MOBA_BLOCK)
    km_p = _pad_rows(km_p, LANES).astype(BF16)
    cqm_p = 256
    oz_b = moba_attention(moba_b3, 0, z_p[None], w_a // LANES, km_p, moba_b3, w_b // LANES,
                          2 * w_b // LANES, cq=cqm_p, tk=tk_p, qpos0=0)
    y_prompt = finish(oz_a[0], oz_b[0], gm_p, xp, slice(0, 1), 256).reshape(1, t_p, d_model)
    new_nsa_prompt = nsa_f[:, w_a:w_a + 4 * kv_a].reshape(1, 1, t_p, 4, NSA_KV_HEADS, HEAD_DIM)
    new_moba_prompt = moba_f[:, w_b:].reshape(1, 1, t_p, 2, MOBA_HEADS, HEAD_DIM)
    wlen = min(WINDOW, t_p)
    new_win_prompt = nsa_f[t_p - wlen:, w_a + 4 * kv_a:].reshape(1, 1, wlen, 2, NSA_KV_HEADS, HEAD_DIM)

    xs = x_sample.reshape(bs, d_model)
    pos_s = jnp.full((bs,), past, jnp.int32)
    nsa_fs, _, moba_fs, _, z_s, gm_s, ga_s = project(xs, slice(1, 1 + bs), pos_s, bs)
    l_s = past + 1
    n_cmp_s = (l_s - CMP_BLOCK) // CMP_STRIDE + 1
    n_sub_s = past // CMP_STRIDE
    assert n_cmp_s == n_sub_s - 1 and n_sub_s % 8 == 0
    assert past % MOBA_BLOCK == 0 and MOBA_BLOCK % page == 0 and page % SEL_BLOCK == 0
    assert wb == WINDOW and wb % 2 == 0 and past >= WINDOW
    n_phys = cache_nsa.shape[1]
    nsa_view = cache_nsa.reshape(n_phys, page, 4 * NSA_KV_HEADS, HEAD_DIM)
    moba_view = cache_moba.reshape(n_phys, page, 2 * MOBA_HEADS, HEAD_DIM)
    win_view = state_nsa_win.reshape(bs, wb // 2, 4 * NSA_KV_HEADS, HEAD_DIM)

    def head_rows(a2d):
        a = a2d.reshape(bs, -1, HEAD_DIM)
        return jnp.pad(a, ((0, 0), (0, HEAD_ROWS - a.shape[1]), (0, 0)))

    qa16 = head_rows(nsa_fs[:, :w_a])
    qb16 = head_rows(moba_fs[:, :w_b])
    za16 = head_rows(z_s[:, :w_a])
    zb16 = head_rows(z_s[:, w_a:])
    ga16 = ga_s.reshape(bs, NSA_KV_HEADS, LANES)[:, :, :3 * NSA_GROUP].reshape(bs, NSA_HEADS, 3)
    ga16 = jnp.pad(ga16, ((0, 0), (0, HEAD_ROWS - NSA_HEADS), (0, LANES - 3)))
    new_a16 = head_rows(nsa_fs[:, w_a + 2 * kv_a:])
    new_b16 = head_rows(moba_fs[:, w_b:])

    n_sel_blk = -(-(past // SEL_BLOCK + 1) // LANES) * LANES
    pool_s = _pool_matrix(n_sub_s, n_cmp_s, n_sel_blk).T
    ocmp16, imp = decode_cmp(nsa_view, page_table, qa16, pe2, w1, w2, pool_s, n_cmp=n_cmp_s, qpos=past)
    kmean_s = decode_kmean(moba_view, page_table,
                           blocks_per_step=8 if (past // MOBA_BLOCK) % 8 == 0 else 1)
    imp_t = imp[:, :NSA_KV_HEADS, :].transpose(2, 0, 1).reshape(n_sel_blk, bs * NSA_KV_HEADS)
    nidx, midx = decode_select(imp_t, kmean_s, qb16, qpos=past)
    oa16, ob16 = decode_attend(nsa_view, moba_view, page_table, nidx, midx,
                               [qa16, qb16, ga16, za16, zb16, ocmp16, new_a16, new_b16], win_view,
                               past=past, win_base=past - wb)
    oz_as = oa16[:, :NSA_HEADS].reshape(bs, w_a)
    oz_bs = ob16[:, :MOBA_HEADS].reshape(bs, w_b)
    y_sample = finish(oz_as, oz_bs, gm_s, xs, slice(1, 1 + bs), bs).reshape(bs, 1, d_model)
    new_nsa_sample = nsa_fs[:, w_a:w_a + 4 * kv_a].reshape(1, bs, 1, 4, NSA_KV_HEADS, HEAD_DIM)
    new_moba_sample = moba_fs[:, w_b:].reshape(1, bs, 1, 2, MOBA_HEADS, HEAD_DIM)
    new_win_sample = jnp.concatenate(
        [state_nsa_win[:, :, 1:],
         nsa_fs[:, w_a + 4 * kv_a:].reshape(1, bs, 1, 2, NSA_KV_HEADS, HEAD_DIM)], axis=2)

    return (y_prompt, y_sample, new_nsa_prompt, new_nsa_sample, new_moba_prompt, new_moba_sample,
            new_win_prompt, new_win_sample)
```

```python
import functools

import numpy as np
import jax
import jax.numpy as jnp
from jax import lax
from jax.experimental import pallas as pl
from jax.experimental.pallas import tpu as pltpu

F32 = jnp.float32
BF16 = jnp.bfloat16

HEAD_DIM = 128
NSA_HEADS = 8
NSA_KV_HEADS = 2
NSA_GROUP = NSA_HEADS // NSA_KV_HEADS
CMP_BLOCK = 32
CMP_STRIDE = 16
SEL_BLOCK = 64
N_SEL = 16
WINDOW = 512
PHI_HIDDEN = 2 * HEAD_DIM
MOBA_HEADS = 8
MOBA_BLOCK = 256
MOBA_TOPK = 3
ROPE_THETA = 10000.0
NORM_EPS = 1e-6

LANES = 128
MASK_BIAS = -(2.0 ** 100)
M_INIT = -1e30
LOG2_E = 1.4426950408889634
VMEM_LIMIT = 56 * 1024 * 1024

NT_DIMS = (((1,), (1,)), ((), ()))
TN_DIMS = (((0,), (0,)), ((), ()))


def _cparams(sem):
    return pltpu.CompilerParams(dimension_semantics=sem, vmem_limit_bytes=VMEM_LIMIT)


def _sigmoid(x):
    return 1.0 / (1.0 + jnp.exp(-x))


def _ada_kernel(c_ref, w_ref, b_ref, o_ref):
    c = c_ref[...]
    a = (c * _sigmoid(c)).astype(BF16)
    o_ref[...] = jnp.dot(a, w_ref[...].astype(BF16), preferred_element_type=F32) + b_ref[...]


def ada_mod(c, w, b, tn=768):
    m, k = c.shape
    n = w.shape[1]
    return pl.pallas_call(
        _ada_kernel,
        out_shape=jax.ShapeDtypeStruct((m, n), F32),
        grid=(n // tn,),
        in_specs=[pl.BlockSpec((m, k), lambda j: (0, 0)),
                  pl.BlockSpec((k, tn), lambda j: (0, j)),
                  pl.BlockSpec((1, tn), lambda j: (0, j))],
        out_specs=pl.BlockSpec((m, tn), lambda j: (0, j)),
        compiler_params=_cparams(("arbitrary",)),
        name="ada_mod",
    )(c, w, b)


def _h_kernel(x_ref, g_ref, sc_ref, sh_ref, o_ref):
    x = x_ref[...]
    ms = jnp.mean(x * x, axis=-1, keepdims=True)
    h = x * lax.rsqrt(ms + NORM_EPS) * g_ref[...]
    o_ref[...] = (h * (1.0 + sc_ref[...]) + sh_ref[...]).astype(BF16)


def norm_modulate(x, g, scale, shift, tm):
    m, d = x.shape
    per_row = scale.shape[0] != 1
    mod_spec = (pl.BlockSpec((tm, d), lambda i: (i, 0)) if per_row
                else pl.BlockSpec((1, d), lambda i: (0, 0)))
    return pl.pallas_call(
        _h_kernel,
        out_shape=jax.ShapeDtypeStruct((m, d), BF16),
        grid=(m // tm,),
        in_specs=[pl.BlockSpec((tm, d), lambda i: (i, 0)),
                  pl.BlockSpec((1, d), lambda i: (0, 0)),
                  mod_spec, mod_spec],
        out_specs=pl.BlockSpec((tm, d), lambda i: (i, 0)),
        compiler_params=_cparams(("arbitrary",)),
        name="norm_modulate",
    )(x, g, scale, shift)


def _proj_rope_kernel(h_ref, w_ref, cos_ref, sin_ref, of_ref, ob_ref, *, full_tiles, partial):
    acc = jnp.dot(h_ref[...], w_ref[...], preferred_element_type=F32)
    all_heads = pl.program_id(1) < full_tiles
    cos = cos_ref[...]
    sin = sin_ref[...]
    for t in range(acc.shape[1] // LANES):
        a = acc[:, t * LANES:(t + 1) * LANES]
        r = a * cos + pltpu.roll(a, HEAD_DIM // 2, axis=1) * sin
        out = r if t < partial else jnp.where(all_heads, r, a)
        of_ref[:, t * LANES:(t + 1) * LANES] = out
        ob_ref[:, t * LANES:(t + 1) * LANES] = out.astype(BF16)


def proj_rope(h, w, cos2, sin2, tm, tn, full_tiles, partial):
    m, k = h.shape
    n = w.shape[1]
    return pl.pallas_call(
        functools.partial(_proj_rope_kernel, full_tiles=full_tiles, partial=partial),
        out_shape=(jax.ShapeDtypeStruct((m, n), F32), jax.ShapeDtypeStruct((m, n), BF16)),
        grid=(m // tm, n // tn),
        in_specs=[pl.BlockSpec((tm, k), lambda i, j: (i, 0)),
                  pl.BlockSpec((k, tn), lambda i, j: (0, j)),
                  pl.BlockSpec((tm, LANES), lambda i, j: (i, 0)),
                  pl.BlockSpec((tm, LANES), lambda i, j: (i, 0))],
        out_specs=(pl.BlockSpec((tm, tn), lambda i, j: (i, j)),
                   pl.BlockSpec((tm, tn), lambda i, j: (i, j))),
        compiler_params=_cparams(("arbitrary", "arbitrary")),
        name="proj_rope",
    )(h, w, cos2, sin2)


def _proj_act_kernel(h_ref, w_ref, o_ref, *, act):
    acc = jnp.dot(h_ref[...], w_ref[...], preferred_element_type=F32)
    s = _sigmoid(acc)
    o_ref[...] = (acc * s if act == "silu" else s).astype(o_ref.dtype)


def proj_act(h, w, act, tm, tn):
    m, k = h.shape
    n = w.shape[1]
    return pl.pallas_call(
        functools.partial(_proj_act_kernel, act=act),
        out_shape=jax.ShapeDtypeStruct((m, n), F32),
        grid=(m // tm, n // tn),
        in_specs=[pl.BlockSpec((tm, k), lambda i, j: (i, 0)),
                  pl.BlockSpec((k, tn), lambda i, j: (0, j))],
        out_specs=pl.BlockSpec((tm, tn), lambda i, j: (i, j)),
        compiler_params=_cparams(("arbitrary", "arbitrary")),
        name="proj_" + act,
    )(h, w)


def _compress_kernel(s_ref, pe_ref, w1_ref, w2_ref, o_ref):
    s = s_ref[0, 0]
    half = s.shape[1]
    top = jnp.dot((s + pe_ref[0, 0:1]).astype(BF16), w1_ref[0, :half], preferred_element_type=F32)
    bot = jnp.dot((s + pe_ref[0, 1:2]).astype(BF16), w1_ref[0, half:], preferred_element_type=F32)
    hid = top + pltpu.roll(bot, s.shape[0] - 1, axis=0)
    hid = (hid * _sigmoid(hid)).astype(BF16)
    o_ref[0, 0] = jnp.dot(hid, w2_ref[0], preferred_element_type=F32).astype(BF16)


def compress(sub, pe2, w1, w2):
    b, four, n_sub, width = sub.shape
    return pl.pallas_call(
        _compress_kernel,
        out_shape=jax.ShapeDtypeStruct((b, four, n_sub, HEAD_DIM), BF16),
        grid=(four, b),
        in_specs=[pl.BlockSpec((1, 1, n_sub, width), lambda j, i: (i, j, 0, 0)),
                  pl.BlockSpec((1, 2, width), lambda j, i: (j // 2, 0, 0)),
                  pl.BlockSpec((1, 2 * width, PHI_HIDDEN), lambda j, i: (j // 2, 0, 0)),
                  pl.BlockSpec((1, PHI_HIDDEN, HEAD_DIM), lambda j, i: (j // 2, 0, 0))],
        out_specs=pl.BlockSpec((1, 1, n_sub, HEAD_DIM), lambda j, i: (i, j, 0, 0)),
        compiler_params=_cparams(("arbitrary", "arbitrary")),
        name="compress",
    )(sub, pe2, w1, w2)


def _kmean_kernel(k_ref, o_ref):
    k = k_ref[0]
    nb = k.shape[0] // MOBA_BLOCK
    o_ref[0] = jnp.mean(k.reshape(nb, MOBA_BLOCK, k.shape[1]), axis=1)


def block_means(rows, col_block, width, blocks_per_step=8):
    b, t, _ = rows.shape
    nb = t // MOBA_BLOCK
    return pl.pallas_call(
        _kmean_kernel,
        out_shape=jax.ShapeDtypeStruct((b, nb, width), F32),
        grid=(b, nb // blocks_per_step),
        in_specs=[pl.BlockSpec((1, blocks_per_step * MOBA_BLOCK, width),
                               lambda i, j: (i, j, col_block))],
        out_specs=pl.BlockSpec((1, blocks_per_step, width), lambda i, j: (i, j, 0)),
        compiler_params=_cparams(("arbitrary", "arbitrary")),
        name="block_means",
    )(rows)


def _masked_softmax_cols(s, valid, scale):
    s = jnp.where(valid, s, -jnp.inf)
    m = jnp.max(s, axis=0, keepdims=True)
    m = jnp.where(m > -jnp.inf, m, 0.0)
    p = jnp.exp2((s - m) * (scale * LOG2_E))
    return p / jnp.maximum(jnp.sum(p, axis=0, keepdims=True), 1e-30)


def _transpose(x):
    rows, cols = x.shape
    pr, pc = -rows % LANES, -cols % LANES
    if pr:
        x = jnp.concatenate([x, jnp.zeros((pr, cols), x.dtype)], axis=0)
    if pc:
        x = jnp.concatenate([x, jnp.zeros((rows + pr, pc), x.dtype)], axis=1)
    return x.T[:cols, :rows]


def _select_top(score, blk, n_iter, limit):
    sel = jnp.zeros(score.shape, jnp.bool_)
    s = score
    blk = blk.astype(F32)
    n_blk = float(score.shape[0])
    for it in range(n_iter):
        mx = jnp.max(s, axis=0, keepdims=True)
        idx = jnp.min(jnp.where(s == mx, blk, n_blk), axis=0, keepdims=True)
        pick = blk == idx
        ok = mx > -jnp.inf
        if limit is not None:
            ok = ok & (limit > it)
        sel = sel | (pick & ok)
        s = jnp.where(pick, -jnp.inf, s)
    return sel


def _block_sparse_flash(qp, k_ref, v_ref, tq_lane, cs, tk, blk_shift, scale):
    r = qp.shape[0]
    row = lax.broadcasted_iota(jnp.int32, (tk, LANES), 0)
    lane = lax.broadcasted_iota(jnp.int32, (tk, LANES), 1)
    rel_blk = lane - (row >> blk_shift)
    blocks_per_tile = tk >> blk_shift

    def scores(i):
        start = pl.multiple_of(i * tk, tk)
        k_t = k_ref[0, pl.ds(start, tk), :]
        onehot = jnp.where(rel_blk == i * blocks_per_tile, 1.0, 0.0).astype(BF16)
        kp = jnp.concatenate([k_t, onehot], axis=1)
        return lax.dot_general(kp, qp, NT_DIMS, preferred_element_type=F32)

    c = scale * LOG2_E

    def update(carry, s, i):
        m, l, acc = carry
        m_new = jnp.maximum(m, jnp.max(s, axis=0, keepdims=True))
        alpha = jnp.exp2((m - m_new) * c)
        p = jnp.exp2((s - m_new) * c)
        l = alpha * l + jnp.sum(p, axis=0, keepdims=True)
        start = pl.multiple_of(i * tk, tk)
        v_t = v_ref[0, pl.ds(start, tk), :]
        pv = lax.dot_general(v_t, p.astype(BF16), TN_DIMS, preferred_element_type=F32)
        return m_new, l, alpha * acc + pv

    def body(i, carry):
        return update(carry, scores(i), i)

    init = (jnp.full((1, r), M_INIT, F32), jnp.zeros((1, r), F32), jnp.zeros((HEAD_DIM, r), F32))
    n_full = cs // tk
    carry = lax.fori_loop(0, n_full, body, init)
    kpos = n_full * tk + lax.broadcasted_iota(jnp.int32, (tk, 1), 0)
    s = jnp.where(kpos <= tq_lane, scores(n_full), M_INIT)
    _, l, acc = update(carry, s, n_full)
    return acc / l


def _nsa_kernel(q_ref, ga_ref, z_ref, kc_ref, vc_ref, ks_ref, vs_ref, kw_ref, vw_ref, pool_ref,
                o_ref, *, cq, tk, qpos0, n_cmp, win_base):
    c = pl.program_id(2)
    r = NSA_GROUP * cq
    cs = qpos0 + c * cq
    scale = HEAD_DIM ** -0.5
    q = q_ref[0]
    qs = jnp.concatenate([q[:, h * LANES:(h + 1) * LANES] for h in range(NSA_GROUP)], axis=0)
    tq_lane = cs + (lax.broadcasted_iota(jnp.int32, (1, r), 1) & (cq - 1))

    kc = kc_ref[0, 0]
    nc_pad = kc.shape[0]
    n_id = lax.broadcasted_iota(jnp.int32, (nc_pad, 1), 0)
    sc = lax.dot_general(kc, qs, NT_DIMS, preferred_element_type=F32)
    cmp_end = jnp.where(n_id < n_cmp, n_id * CMP_STRIDE + (CMP_BLOCK - 1), jnp.iinfo(jnp.int32).max)
    pc = _masked_softmax_cols(sc, cmp_end <= tq_lane, scale)
    o_cmp = lax.dot_general(vc_ref[0, 0], pc.astype(BF16), TN_DIMS, preferred_element_type=F32)

    psum = pc[:, 0:cq]
    for h in range(1, NSA_GROUP):
        psum = psum + pc[:, h * cq:(h + 1) * cq]
    hi = psum.astype(BF16)
    r1 = psum - hi.astype(F32)
    mid = r1.astype(BF16)
    lo = (r1 - mid.astype(F32)).astype(BF16)
    imp = jnp.dot(pool_ref[...], jnp.concatenate([hi, mid, lo], axis=0), preferred_element_type=F32)

    blk = lax.broadcasted_iota(jnp.int32, (LANES, 1), 0)
    tq = cs + lax.broadcasted_iota(jnp.int32, (1, cq), 1)
    own = tq >> 6
    open_blk = blk * SEL_BLOCK <= tq
    forced = open_blk & ((blk == 0) | (blk == own) | (blk == own - 1))
    score = jnp.where(forced, jnp.inf, jnp.where(open_blk, imp, -jnp.inf))
    n_beyond = (own >= LANES).astype(jnp.int32) + (own - 1 >= LANES).astype(jnp.int32)
    sel = _select_top(score, blk, N_SEL, N_SEL - n_beyond)
    bias = _transpose(jnp.where(sel, 0.0, MASK_BIAS)).astype(BF16)
    qp = jnp.concatenate([qs, jnp.concatenate([bias] * NSA_GROUP, axis=0)], axis=1)
    o_slc = _block_sparse_flash(qp, ks_ref, vs_ref, tq_lane, cs, tk, 6, scale)

    tw = WINDOW + cq
    w_start = pl.multiple_of(c * cq, cq)
    k_w = kw_ref[0, pl.ds(w_start, tw), :]
    v_w = vw_ref[0, pl.ds(w_start, tw), :]
    wpos = (cs - WINDOW) + lax.broadcasted_iota(jnp.int32, (tw, 1), 0)
    sw = lax.dot_general(k_w, qs, NT_DIMS, preferred_element_type=F32)
    wpos = jnp.where(wpos >= win_base, wpos, jnp.iinfo(jnp.int32).min // 2)
    wvalid = (tq_lane - wpos).astype(jnp.uint32) < WINDOW
    pw = _masked_softmax_cols(sw, wvalid, scale)
    o_win = lax.dot_general(v_w, pw.astype(BF16), TN_DIMS, preferred_element_type=F32)

    g = _transpose(ga_ref[0])
    for h in range(NSA_GROUP):
        cols = slice(h * cq, (h + 1) * cq)
        o = (g[3 * h:3 * h + 1] * o_cmp[:, cols] + g[3 * h + 1:3 * h + 2] * o_slc[:, cols]
             + g[3 * h + 2:3 * h + 3] * o_win[:, cols])
        o_ref[0, :, h * LANES:(h + 1) * LANES] = (
            _transpose(o) * z_ref[0, :, h * LANES:(h + 1) * LANES]).astype(BF16)


def nsa_attention(q, q_col, gates, z, z_col, kc, ks, ks_col, vs_col, kw, kw_col, vw_col, pool,
                  *, cq, tk, qpos0, n_cmp, win_base):
    b, tq_len, _ = q.shape
    tkv = ks.shape[1]
    tw_len = kw.shape[1]
    nc_pad = kc.shape[2]
    gw = NSA_GROUP * LANES
    kern = functools.partial(_nsa_kernel, cq=cq, tk=tk, qpos0=qpos0, n_cmp=n_cmp, win_base=win_base)
    return pl.pallas_call(
        kern,
        out_shape=jax.ShapeDtypeStruct((b, tq_len, NSA_HEADS * HEAD_DIM), BF16),
        grid=(b, NSA_KV_HEADS, tq_len // cq),
        in_specs=[
            pl.BlockSpec((1, cq, gw), lambda i, g, c: (i, c, q_col + g)),
            pl.BlockSpec((1, cq, LANES), lambda i, g, c: (i, c, g)),
            pl.BlockSpec((1, cq, gw), lambda i, g, c: (i, c, z_col + g)),
            pl.BlockSpec((1, 1, nc_pad, HEAD_DIM), lambda i, g, c: (i, g, 0, 0)),
            pl.BlockSpec((1, 1, nc_pad, HEAD_DIM), lambda i, g, c: (i, 2 + g, 0, 0)),
            pl.BlockSpec((1, tkv, HEAD_DIM), lambda i, g, c: (i, 0, ks_col + g)),
            pl.BlockSpec((1, tkv, HEAD_DIM), lambda i, g, c: (i, 0, vs_col + g)),
            pl.BlockSpec((1, tw_len, HEAD_DIM), lambda i, g, c: (i, 0, kw_col + g)),
            pl.BlockSpec((1, tw_len, HEAD_DIM), lambda i, g, c: (i, 0, vw_col + g)),
            pl.BlockSpec(pool.shape, lambda i, g, c: (0, 0)),
        ],
        out_specs=pl.BlockSpec((1, cq, gw), lambda i, g, c: (i, c, g)),
        compiler_params=_cparams(("arbitrary", "arbitrary", "arbitrary")),
        name="nsa_attention",
    )(q, gates, z, kc, kc, ks, ks, kw, kw, pool)


def _moba_kernel(q_ref, z_ref, km_ref, k_ref, v_ref, o_ref, *, cq, tk, qpos0):
    c = pl.program_id(2)
    cs = qpos0 + c * cq
    scale = HEAD_DIM ** -0.5
    q = q_ref[0]
    tq = cs + lax.broadcasted_iota(jnp.int32, (1, cq), 1)
    blk = lax.broadcasted_iota(jnp.int32, (LANES, 1), 0)
    own = tq >> 8
    sc = lax.dot_general(km_ref[0], q, NT_DIMS, preferred_element_type=F32)
    score = jnp.where(blk < own, sc, -jnp.inf)
    sel = _select_top(score, blk, MOBA_TOPK, None) | (blk == own)
    bias = _transpose(jnp.where(sel, 0.0, MASK_BIAS)).astype(BF16)
    qp = jnp.concatenate([q, bias], axis=1)
    o = _block_sparse_flash(qp, k_ref, v_ref, tq, cs, tk, 8, scale)
    o_ref[0] = (_transpose(o) * z_ref[0]).astype(BF16)


def moba_attention(q, q_col, z, z_col, kmean, k, k_col, v_col, *, cq, tk, qpos0):
    b, tq_len, _ = q.shape
    tkv = k.shape[1]
    kern = functools.partial(_moba_kernel, cq=cq, tk=tk, qpos0=qpos0)
    return pl.pallas_call(
        kern,
        out_shape=jax.ShapeDtypeStruct((b, tq_len, MOBA_HEADS * HEAD_DIM), BF16),
        grid=(b, MOBA_HEADS, tq_len // cq),
        in_specs=[
            pl.BlockSpec((1, cq, LANES), lambda i, h, c: (i, c, q_col + h)),
            pl.BlockSpec((1, cq, LANES), lambda i, h, c: (i, c, z_col + h)),
            pl.BlockSpec((1, LANES, HEAD_DIM), lambda i, h, c: (i, 0, h)),
            pl.BlockSpec((1, tkv, HEAD_DIM), lambda i, h, c: (i, 0, k_col + h)),
            pl.BlockSpec((1, tkv, HEAD_DIM), lambda i, h, c: (i, 0, v_col + h)),
        ],
        out_specs=pl.BlockSpec((1, cq, LANES), lambda i, h, c: (i, c, h)),
        compiler_params=_cparams(("arbitrary", "arbitrary", "arbitrary")),
        name="moba_attention",
    )(q, z, kmean, k, k)


def _merge_kernel(oa_ref, ob_ref, ga_ref, gb_ref, wa_ref, wb_ref, o_ref):
    br_a = jnp.dot(oa_ref[...], wa_ref[...], preferred_element_type=F32)
    br_b = jnp.dot(ob_ref[...], wb_ref[...], preferred_element_type=F32)
    o_ref[...] = (ga_ref[...] * br_a + gb_ref[...] * br_b).astype(BF16)


def merge_branches(oz_a, oz_b, g_m, w_a, w_b, tm):
    m, ka = oz_a.shape
    d = w_a.shape[1]
    return pl.pallas_call(
        _merge_kernel,
        out_shape=jax.ShapeDtypeStruct((m, d), BF16),
        grid=(m // tm,),
        in_specs=[pl.BlockSpec((tm, ka), lambda i: (i, 0)),
                  pl.BlockSpec((tm, ka), lambda i: (i, 0)),
                  pl.BlockSpec((tm, d), lambda i: (i, 0)),
                  pl.BlockSpec((tm, d), lambda i: (i, 1)),
                  pl.BlockSpec((ka, d), lambda i: (0, 0)),
                  pl.BlockSpec((ka, d), lambda i: (0, 0))],
        out_specs=pl.BlockSpec((tm, d), lambda i: (i, 0)),
        compiler_params=_cparams(("arbitrary",)),
        name="merge_branches",
    )(oz_a, oz_b, g_m, g_m, w_a, w_b)


def _final_kernel(m_ref, x_ref, gate_ref, w_ref, g_ref, o_ref):
    y = x_ref[...] + gate_ref[...] * jnp.dot(m_ref[...], w_ref[...], preferred_element_type=F32)
    ms = jnp.mean(y * y, axis=-1, keepdims=True)
    o_ref[...] = y * lax.rsqrt(ms + NORM_EPS) * g_ref[...]


def final_out(merged, x, gate, w, g, tm):
    m, d = x.shape
    per_row = gate.shape[0] != 1
    gate_spec = (pl.BlockSpec((tm, d), lambda i: (i, 0)) if per_row
                 else pl.BlockSpec((1, d), lambda i: (0, 0)))
    return pl.pallas_call(
        _final_kernel,
        out_shape=jax.ShapeDtypeStruct((m, d), F32),
        grid=(m // tm,),
        in_specs=[pl.BlockSpec((tm, d), lambda i: (i, 0)),
                  pl.BlockSpec((tm, d), lambda i: (i, 0)),
                  gate_spec,
                  pl.BlockSpec((d, d), lambda i: (0, 0)),
                  pl.BlockSpec((1, d), lambda i: (0, 0))],
        out_specs=pl.BlockSpec((tm, d), lambda i: (i, 0)),
        compiler_params=_cparams(("arbitrary",)),
        name="final_out",
    )(merged, x, gate, w, g)


HEAD_ROWS = 16


def _softmax_rows_with_extra(s, valid, s_new):
    s = jnp.where(valid, s, -jnp.inf)
    m = jnp.maximum(jnp.max(s, axis=1, keepdims=True), s_new)
    p = jnp.exp(s - m)
    p_new = jnp.exp(s_new - m)
    l = jnp.sum(p, axis=1, keepdims=True) + p_new
    return p / l, p_new / l


def _bf16_round(x):
    return x.astype(BF16).astype(F32)


def _decode_cmp_kernel(pt_ref, cache_ref, q_ref, pe_ref, w1_ref, w2_ref, pool_ref,
                       ocmp_ref, imp_ref, bufk, bufv, xtop, xbot, sem,
                       *, n_pages, page, n_cmp, qpos):
    s = pl.program_id(0)
    n_seq = pl.num_programs(0)
    n_sub = n_pages * page // CMP_STRIDE
    half = CMP_STRIDE * HEAD_DIM
    scale = HEAD_DIM ** -0.5

    def page_copies(seq, kv, buf, fn):
        def body(p, c):
            pid = pt_ref[seq, p]
            for g in range(NSA_KV_HEADS):
                fn(pltpu.make_async_copy(cache_ref.at[pid, :, kv * NSA_KV_HEADS + g, :],
                                         buf.at[g, pl.ds(p * page, page), :], sem.at[kv]), g)
            return c
        lax.fori_loop(0, n_pages, body, 0)

    def start(cp, lane):
        cp.start(priority=lane % 2)

    def wait(cp, lane):
        cp.wait()

    def compress_half(buf, kv):
        for r in range(CMP_STRIDE):
            x = jnp.concatenate([buf[g, pl.ds(r, n_sub, stride=CMP_STRIDE), :]
                                 for g in range(NSA_KV_HEADS)], axis=0)
            cols = slice(r * HEAD_DIM, (r + 1) * HEAD_DIM)
            xtop[:, cols] = (x + pe_ref[kv, 0:1, cols]).astype(BF16)
            xbot[:, cols] = (x + pe_ref[kv, 1:2, cols]).astype(BF16)
        top = jnp.dot(xtop[...], w1_ref[kv, :half], preferred_element_type=F32)
        bot = jnp.dot(xbot[...], w1_ref[kv, half:], preferred_element_type=F32)
        outs = []
        for g in range(NSA_KV_HEADS):
            rows = slice(g * n_sub, (g + 1) * n_sub)
            hid = top[rows] + pltpu.roll(bot[rows], n_sub - 1, axis=0)
            hid = (hid * _sigmoid(hid)).astype(BF16)
            outs.append(jnp.dot(hid, w2_ref[kv], preferred_element_type=F32).astype(BF16))
        return outs

    @pl.when(s == 0)
    def _():
        page_copies(0, 0, bufk, start)
        page_copies(0, 1, bufv, start)

    page_copies(s, 0, bufk, wait)
    kc = compress_half(bufk, 0)

    @pl.when(s + 1 < n_seq)
    def _():
        page_copies(s + 1, 0, bufk, start)

    page_copies(s, 1, bufv, wait)
    vc = compress_half(bufv, 1)

    @pl.when(s + 1 < n_seq)
    def _():
        page_copies(s + 1, 1, bufv, start)

    q = q_ref[0].astype(BF16)
    row_g = lax.broadcasted_iota(jnp.int32, (HEAD_ROWS, 1), 0) >> 2
    sc = [lax.dot_general(q, kc[g], NT_DIMS, preferred_element_type=F32) for g in range(NSA_KV_HEADS)]
    sc = jnp.where(row_g == 0, sc[0], sc[1]) * scale
    n_id = lax.broadcasted_iota(jnp.int32, (1, n_sub), 1)
    valid = ((n_id * CMP_STRIDE + (CMP_BLOCK - 1)) <= qpos) & (n_id < n_cmp)
    sc = jnp.where(valid, sc, -jnp.inf)
    m = jnp.max(sc, axis=1, keepdims=True)
    m = jnp.where(m > -jnp.inf, m, 0.0)
    p = jnp.exp(sc - m)
    pc = p / jnp.maximum(jnp.sum(p, axis=1, keepdims=True), 1e-30)
    pcb = pc.astype(BF16)
    o = [jnp.dot(pcb, vc[g], preferred_element_type=F32) for g in range(NSA_KV_HEADS)]
    ocmp_ref[0] = jnp.where(row_g == 0, o[0], o[1])
    psum = jnp.concatenate(
        [jnp.sum(pc[g * NSA_GROUP:(g + 1) * NSA_GROUP], axis=0, keepdims=True) for g in range(NSA_KV_HEADS)]
        + [jnp.zeros((HEAD_ROWS - NSA_KV_HEADS, n_sub), F32)], axis=0)
    hi = psum.astype(BF16)
    r1 = psum - hi.astype(F32)
    mid = r1.astype(BF16)
    lo = (r1 - mid.astype(F32)).astype(BF16)
    imp_ref[0] = jnp.dot(jnp.concatenate([hi, mid, lo], axis=1), pool_ref[...], preferred_element_type=F32)


def decode_cmp(cache_view, page_table, q16, pe2, w1, w2, pool_nat, *, n_cmp, qpos):
    bs, n_pages = page_table.shape
    page = cache_view.shape[1]
    past = n_pages * page
    n_sub = past // CMP_STRIDE
    nb = pool_nat.shape[1]
    half = CMP_STRIDE * HEAD_DIM

    def whole(a):
        return pl.BlockSpec(a.shape, lambda i, pt: (0,) * a.ndim)

    grid_spec = pltpu.PrefetchScalarGridSpec(
        num_scalar_prefetch=1,
        grid=(bs,),
        in_specs=[pl.BlockSpec(memory_space=pl.ANY),
                  pl.BlockSpec((1, HEAD_ROWS, HEAD_DIM), lambda i, pt: (i, 0, 0)),
                  whole(pe2), whole(w1), whole(w2), whole(pool_nat)],
        out_specs=(pl.BlockSpec((1, HEAD_ROWS, HEAD_DIM), lambda i, pt: (i, 0, 0)),
                   pl.BlockSpec((1, HEAD_ROWS, nb), lambda i, pt: (i, 0, 0))),
        scratch_shapes=[pltpu.VMEM((NSA_KV_HEADS, past, HEAD_DIM), F32),
                        pltpu.VMEM((NSA_KV_HEADS, past, HEAD_DIM), F32),
                        pltpu.VMEM((NSA_KV_HEADS * n_sub, half), BF16),
                        pltpu.VMEM((NSA_KV_HEADS * n_sub, half), BF16),
                        pltpu.SemaphoreType.DMA((2,))],
    )
    return pl.pallas_call(
        functools.partial(_decode_cmp_kernel, n_pages=n_pages, page=page, n_cmp=n_cmp, qpos=qpos),
        out_shape=(jax.ShapeDtypeStruct((bs, HEAD_ROWS, HEAD_DIM), F32),
                   jax.ShapeDtypeStruct((bs, HEAD_ROWS, nb), F32)),
        grid_spec=grid_spec,
        compiler_params=_cparams(("arbitrary",)),
        name="decode_cmp",
    )(page_table, cache_view, q16, pe2, w1, w2, pool_nat)


def _decode_kmean_kernel(pt_ref, *refs, n_in, pages_per_block):
    o_ref = refs[n_in]
    for b in range(n_in // pages_per_block):
        tot = jnp.sum(refs[b * pages_per_block][0], axis=0)
        for pp in range(1, pages_per_block):
            tot = tot + jnp.sum(refs[b * pages_per_block + pp][0], axis=0)
        o_ref[0, b] = tot / MOBA_BLOCK


def decode_kmean(cache_view, page_table, blocks_per_step):
    bs, n_pages = page_table.shape
    page = cache_view.shape[1]
    ppb = MOBA_BLOCK // page
    n_blocks = n_pages // ppb
    n_in = blocks_per_step * ppb
    in_specs = [pl.BlockSpec((1, page, MOBA_HEADS, HEAD_DIM),
                             lambda i, j, pt, k=k: (pt[i, j * n_in + k], 0, 0, 0)) for k in range(n_in)]
    grid_spec = pltpu.PrefetchScalarGridSpec(
        num_scalar_prefetch=1,
        grid=(bs, n_blocks // blocks_per_step),
        in_specs=in_specs,
        out_specs=pl.BlockSpec((1, blocks_per_step, MOBA_HEADS, HEAD_DIM), lambda i, j, pt: (i, j, 0, 0)),
    )
    return pl.pallas_call(
        functools.partial(_decode_kmean_kernel, n_in=n_in, pages_per_block=ppb),
        out_shape=jax.ShapeDtypeStruct((bs, n_blocks, MOBA_HEADS, HEAD_DIM), F32),
        grid_spec=grid_spec,
        compiler_params=_cparams(("arbitrary", "arbitrary")),
        name="decode_kmean",
    )(page_table, *([cache_view] * n_in))


def _top_indices(score, blk, n_iter):
    rows = []
    s = score
    blk = blk.astype(F32)
    n_blk = float(score.shape[0])
    for _ in range(n_iter):
        mx = jnp.max(s, axis=0, keepdims=True)
        idx = jnp.min(jnp.where(s == mx, blk, n_blk), axis=0, keepdims=True)
        rows.append(jnp.where(mx > -jnp.inf, idx, -1.0))
        s = jnp.where(blk == idx, -jnp.inf, s)
    return rows


def _decode_select_kernel(imp_ref, km_ref, q_ref, nidx_ref, midx_ref, *, qpos):
    nbk = imp_ref.shape[0]
    blk = lax.broadcasted_iota(jnp.int32, (nbk, 1), 0)
    own = qpos // SEL_BLOCK
    open_blk = blk * SEL_BLOCK <= qpos
    forced = open_blk & ((blk == 0) | (blk == own) | (blk == own - 1))
    score = jnp.where(forced, jnp.inf, jnp.where(open_blk, imp_ref[...], -jnp.inf))
    nidx_ref[...] = jnp.concatenate(_top_indices(score, blk, N_SEL), axis=0).astype(jnp.int32)

    n_seq, n_blocks = km_ref.shape[0], km_ref.shape[1]
    n_col = n_seq * MOBA_HEADS
    lane = lax.broadcasted_iota(jnp.int32, (1, n_col), 1)

    def body(i, sct):
        q = _bf16_round(q_ref[i])
        for h in range(MOBA_HEADS):
            prod = _bf16_round(km_ref[i, :, h, :]) * q[h:h + 1]
            col = jnp.sum(prod, axis=1, keepdims=True)
            sct = jnp.where(lane == i * MOBA_HEADS + h, col, sct)
        return sct

    sct = lax.fori_loop(0, n_seq, body, jnp.zeros((n_blocks, n_col), F32))
    blk_m = lax.broadcasted_iota(jnp.int32, (n_blocks, 1), 0)
    score_m = jnp.where(blk_m < qpos // MOBA_BLOCK, sct, -jnp.inf)
    rows = _top_indices(score_m, blk_m, MOBA_TOPK)
    rows = rows + [jnp.full((1, n_col), -1.0, F32)] * (midx_ref.shape[0] - MOBA_TOPK)
    midx_ref[...] = jnp.concatenate(rows, axis=0).astype(jnp.int32)


def decode_select(imp_t, kmean, q16, *, qpos):
    n_col_a = imp_t.shape[1]
    n_col_b = kmean.shape[0] * MOBA_HEADS

    def whole(a):
        return pl.BlockSpec(a.shape, lambda i: (0,) * a.ndim)

    return pl.pallas_call(
        functools.partial(_decode_select_kernel, qpos=qpos),
        out_shape=(jax.ShapeDtypeStruct((N_SEL, n_col_a), jnp.int32),
                   jax.ShapeDtypeStruct((8, n_col_b), jnp.int32)),
        grid=(1,),
        in_specs=[whole(imp_t), whole(kmean), whole(q16)],
        out_specs=(pl.BlockSpec((N_SEL, n_col_a), lambda i: (0, 0)),
                   pl.BlockSpec((8, n_col_b), lambda i: (0, 0))),
        compiler_params=_cparams(("arbitrary",)),
        name="decode_select",
    )(imp_t, kmean, q16)


def _decode_attend_kernel(pt_ref, nidx_ref, midx_ref, nsa_ref, moba_ref,
                          qa_ref, qb_ref, ga_ref, za_ref, zb_ref, ocmp_ref, newa_ref, newb_ref, win_ref,
                          oa_ref, ob_ref, ks, vs, km, vm, sem, *, page, past, win_base):
    s = pl.program_id(0)
    n_seq = pl.num_programs(0)
    slot = lax.rem(s, 2)
    scale = HEAD_DIM ** -0.5
    blocks_per_page = page // SEL_BLOCK
    pages_per_block = MOBA_BLOCK // page
    n_cache_blk = past // SEL_BLOCK

    def nsa_copies(seq, sl, fn):
        for g in range(NSA_KV_HEADS):
            def body(i, c, g=g):
                blk = jnp.clip(nidx_ref[i, seq * NSA_KV_HEADS + g], 0, n_cache_blk - 1)
                pid = pt_ref[seq, blk // blocks_per_page]
                rows = pl.ds((blk % blocks_per_page) * SEL_BLOCK, SEL_BLOCK)
                dst = pl.ds(i * SEL_BLOCK, SEL_BLOCK)
                fn(pltpu.make_async_copy(nsa_ref.at[pid, rows, 2 * NSA_KV_HEADS + g, :],
                                         ks.at[sl, g, dst, :], sem.at[sl, 0]), 0)
                fn(pltpu.make_async_copy(nsa_ref.at[pid, rows, 3 * NSA_KV_HEADS + g, :],
                                         vs.at[sl, g, dst, :], sem.at[sl, 0]), 1)
                return c
            lax.fori_loop(0, N_SEL, body, 0)

    def moba_copies(seq, sl, fn):
        for h in range(MOBA_HEADS):
            def body(i, c, h=h):
                blk = jnp.maximum(midx_ref[i, seq * MOBA_HEADS + h], 0)
                for pp in range(pages_per_block):
                    pid = pt_ref[seq, blk * pages_per_block + pp]
                    dst = pl.ds((i * pages_per_block + pp) * page, page)
                    fn(pltpu.make_async_copy(moba_ref.at[pid, :, h, :], km.at[sl, h, dst, :], sem.at[sl, 1]), 0)
                    fn(pltpu.make_async_copy(moba_ref.at[pid, :, MOBA_HEADS + h, :],
                                             vm.at[sl, h, dst, :], sem.at[sl, 1]), 1)
                return c
            lax.fori_loop(0, MOBA_TOPK, body, 0)

    def start(cp, lane):
        cp.start(priority=lane % 2)

    def wait(cp, lane):
        cp.wait()

    @pl.when(s == 0)
    def _():
        nsa_copies(0, 0, start)
        moba_copies(0, 0, start)

    @pl.when(s + 1 < n_seq)
    def _():
        nsa_copies(s + 1, 1 - slot, start)
        moba_copies(s + 1, 1 - slot, start)

    nsa_copies(s, slot, wait)
    moba_copies(s, slot, wait)

    row = lax.broadcasted_iota(jnp.int32, (HEAD_ROWS, 1), 0)
    row_g = row >> 2
    qa = qa_ref[0].astype(BF16)
    qa_f = qa.astype(F32)
    new_a = _bf16_round(newa_ref[0])

    n_keys = N_SEL * SEL_BLOCK
    lane_blk = lax.broadcasted_iota(jnp.int32, (1, n_keys), 1) >> (SEL_BLOCK.bit_length() - 1)
    o_slc = []
    for g in range(NSA_KV_HEADS):
        k = ks[slot, g].astype(BF16)
        v = vs[slot, g].astype(BF16)
        sc = lax.dot_general(qa, k, NT_DIMS, preferred_element_type=F32) * scale
        okv = jnp.zeros((1, n_keys), jnp.int32)
        for i in range(N_SEL):
            b = nidx_ref[i, s * NSA_KV_HEADS + g]
            ok = ((b >= 0) & (b < n_cache_blk)).astype(jnp.int32)
            okv = jnp.where(lane_blk == i, ok, okv)
        s_new = jnp.sum(qa_f * new_a[g:g + 1], axis=1, keepdims=True) * scale
        pn, pn_new = _softmax_rows_with_extra(sc, okv > 0, s_new)
        o_slc.append(jnp.dot(pn.astype(BF16), v, preferred_element_type=F32)
                     + _bf16_round(pn_new) * new_a[NSA_KV_HEADS + g:NSA_KV_HEADS + g + 1])
    o_slc = jnp.where(row_g == 0, o_slc[0], o_slc[1])

    half_w = win_ref.shape[1]
    pos_e = win_base + 2 * lax.broadcasted_iota(jnp.int32, (1, half_w), 1)
    o_win = []
    for g in range(NSA_KV_HEADS):
        parts = []
        for parity in range(2):
            k = win_ref[0, :, 4 * parity + g, :].astype(BF16)
            sc = lax.dot_general(qa, k, NT_DIMS, preferred_element_type=F32) * scale
            pos = pos_e + parity
            ok = (pos <= past) & (pos > past - WINDOW) & (pos >= win_base)
            parts.append(jnp.where(ok, sc, -jnp.inf))
        s_new = jnp.sum(qa_f * new_a[4 + g:5 + g], axis=1, keepdims=True) * scale
        sc = jnp.concatenate(parts, axis=1)
        pn, pn_new = _softmax_rows_with_extra(sc, sc > -jnp.inf, s_new)
        o = _bf16_round(pn_new) * new_a[6 + g:7 + g]
        for parity in range(2):
            v = win_ref[0, :, 4 * parity + 2 + g, :].astype(BF16)
            o = o + jnp.dot(pn[:, parity * half_w:(parity + 1) * half_w].astype(BF16), v,
                            preferred_element_type=F32)
        o_win.append(o)
    o_win = jnp.where(row_g == 0, o_win[0], o_win[1])

    gates = ga_ref[0]
    o_a = gates[:, 0:1] * ocmp_ref[0] + gates[:, 1:2] * o_slc + gates[:, 2:3] * o_win
    oa_ref[0] = (o_a * za_ref[0]).astype(BF16)

    qb = qb_ref[0].astype(BF16)
    qb_f = qb.astype(F32)
    new_b = _bf16_round(newb_ref[0])
    n_keys_b = MOBA_TOPK * MOBA_BLOCK
    lane_blk_b = lax.broadcasted_iota(jnp.int32, (1, n_keys_b), 1) >> (MOBA_BLOCK.bit_length() - 1)
    o_b = jnp.zeros((HEAD_ROWS, HEAD_DIM), F32)
    for h in range(MOBA_HEADS):
        k = km[slot, h].astype(BF16)
        v = vm[slot, h].astype(BF16)
        sc = lax.dot_general(qb, k, NT_DIMS, preferred_element_type=F32) * scale
        okv = jnp.zeros((1, n_keys_b), jnp.int32)
        for i in range(MOBA_TOPK):
            ok = (midx_ref[i, s * MOBA_HEADS + h] >= 0).astype(jnp.int32)
            okv = jnp.where(lane_blk_b == i, ok, okv)
        s_new = jnp.sum(qb_f * new_b[h:h + 1], axis=1, keepdims=True) * scale
        pn, pn_new = _softmax_rows_with_extra(sc, okv > 0, s_new)
        o = (jnp.dot(pn.astype(BF16), v, preferred_element_type=F32)
             + _bf16_round(pn_new) * new_b[MOBA_HEADS + h:MOBA_HEADS + h + 1])
        o_b = jnp.where(row == h, o, o_b)
    ob_ref[0] = (o_b * zb_ref[0]).astype(BF16)


def decode_attend(nsa_view, moba_view, page_table, nidx, midx, per_seq, win_view, *, past, win_base):
    bs = page_table.shape[0]
    page = nsa_view.shape[1]
    row_spec = pl.BlockSpec((1, HEAD_ROWS, HEAD_DIM), lambda i, *_: (i, 0, 0))
    grid_spec = pltpu.PrefetchScalarGridSpec(
        num_scalar_prefetch=3,
        grid=(bs,),
        in_specs=[pl.BlockSpec(memory_space=pl.ANY), pl.BlockSpec(memory_space=pl.ANY)]
        + [row_spec] * len(per_seq)
        + [pl.BlockSpec((1,) + win_view.shape[1:], lambda i, *_: (i, 0, 0, 0))],
        out_specs=(row_spec, row_spec),
        scratch_shapes=[pltpu.VMEM((2, NSA_KV_HEADS, N_SEL * SEL_BLOCK, HEAD_DIM), F32),
                        pltpu.VMEM((2, NSA_KV_HEADS, N_SEL * SEL_BLOCK, HEAD_DIM), F32),
                        pltpu.VMEM((2, MOBA_HEADS, MOBA_TOPK * MOBA_BLOCK, HEAD_DIM), F32),
                        pltpu.VMEM((2, MOBA_HEADS, MOBA_TOPK * MOBA_BLOCK, HEAD_DIM), F32),
                        pltpu.SemaphoreType.DMA((2, 2))],
    )
    return pl.pallas_call(
        functools.partial(_decode_attend_kernel, page=page, past=past, win_base=win_base),
        out_shape=(jax.ShapeDtypeStruct((bs, HEAD_ROWS, HEAD_DIM), BF16),
                   jax.ShapeDtypeStruct((bs, HEAD_ROWS, HEAD_DIM), BF16)),
        grid_spec=grid_spec,
        compiler_params=_cparams(("arbitrary",)),
        name="decode_attend",
    )(page_table, nidx, midx, nsa_view, moba_view, *per_seq, win_view)


def _rope_tables(pos):
    half = HEAD_DIM // 2
    inv = ROPE_THETA ** (-(jnp.arange(half, dtype=F32) / half))
    ang = pos.astype(F32)[:, None] * inv[None, :]
    cos, sin = jnp.cos(ang), jnp.sin(ang)
    return jnp.concatenate([cos, cos], axis=1), jnp.concatenate([-sin, sin], axis=1)


def _pool_matrix(nc_pad, n_cmp, n_blk=LANES):
    ratio = SEL_BLOCK // CMP_STRIDE
    n_sub = CMP_BLOCK // CMP_STRIDE
    j = np.arange(n_blk)[:, None]
    n = np.arange(nc_pad)[None, :]
    pool = ((n >= ratio * j - (n_sub - 1)) & (n <= ratio * j + ratio - 1) & (n < n_cmp)).astype(np.float32)
    return jnp.asarray(np.concatenate([pool, pool, pool], axis=1), dtype=BF16)


def _pad_rows(a, rows):
    return jnp.pad(a, ((0, 0), (0, rows - a.shape[1]), (0, 0)))


def kernel(x_prompt, x_sample, cache_nsa, cache_moba, state_nsa_win, page_table, c_prompt, c_sample,
           w_ada, b_ada, norm_g, w_in, pe_k, w_phi_k1, w_phi_k2, pe_v, w_phi_v1, w_phi_v2,
           w_out_a, w_out_b, w_out, final_g):
    depth = w_in.shape[0]
    assert depth == 1, "single-layer step"
    d_model = x_prompt.shape[-1]
    bp, t_p, _ = x_prompt.shape
    bs, t_s, _ = x_sample.shape
    assert bp == 1 and t_s == 1
    page = cache_nsa.shape[2]
    n_pages = page_table.shape[1]
    past = n_pages * page
    wb = state_nsa_win.shape[2]
    w_a = NSA_HEADS * HEAD_DIM
    w_b = MOBA_HEADS * HEAD_DIM
    kv_a = NSA_KV_HEADS * HEAD_DIM
    s0 = w_a + 6 * kv_a
    s1 = s0 + 3 * NSA_HEADS
    s2 = s1 + w_a
    s3 = s2 + 3 * w_b
    s4 = s3 + w_b

    wi = w_in[0]
    w_nsa = wi[:, :s0].astype(BF16)
    w_moba = wi[:, s2:s3].astype(BF16)
    w_z = jnp.concatenate([wi[:, s1:s2], wi[:, s3:s4]], axis=1).astype(BF16)
    w_gm = wi[:, s4:].astype(BF16)
    w_ga = wi[:, s0:s1].reshape(d_model, NSA_KV_HEADS, 3 * NSA_GROUP)
    w_ga = jnp.pad(w_ga, ((0, 0), (0, 0), (0, LANES - 3 * NSA_GROUP))).reshape(d_model, -1).astype(BF16)
    w1 = jnp.stack([w_phi_k1[0], w_phi_v1[0]]).astype(BF16)
    w2 = jnp.stack([w_phi_k2[0], w_phi_v2[0]]).astype(BF16)
    half_w = CMP_STRIDE * HEAD_DIM
    pe2 = jnp.stack([pe_k[0].reshape(2, half_w), pe_v[0].reshape(2, half_w)])
    woa = w_out_a[0].astype(BF16)
    wob = w_out_b[0].astype(BF16)
    wo = w_out[0].astype(BF16)
    fg = final_g.reshape(1, d_model)
    ng = norm_g[0].reshape(1, d_model)

    c_all = jnp.concatenate([c_prompt, c_sample], axis=0)
    n_c = c_all.shape[0]
    c_all = jnp.pad(c_all, ((0, -n_c % 8), (0, 0)))
    mod = ada_mod(c_all, w_ada[0], b_ada[0].reshape(1, -1))
    shift, scale, gate = mod[:, :d_model], mod[:, d_model:2 * d_model], mod[:, 2 * d_model:]

    def project(x2d, rows, pos, tm):
        sl = slice(rows.start, rows.stop)
        h = norm_modulate(x2d, ng, scale[sl], shift[sl], tm)
        cos2, sin2 = _rope_tables(pos)
        nsa_f, nsa_b = proj_rope(h, w_nsa, cos2, sin2, tm, 512, w_a // 512, 2)
        moba_f, moba_b = proj_rope(h, w_moba, cos2, sin2, tm, 512, 2 * w_b // 512, 0)
        z = proj_act(h, w_z, "silu", tm, 512)
        g_m = proj_act(h, w_gm, "sigmoid", tm, 512)
        g_a = proj_act(h, w_ga, "sigmoid", tm, NSA_KV_HEADS * LANES)
        return nsa_f, nsa_b, moba_f, moba_b, z, g_m, g_a

    def finish(oz_a, oz_b, g_m, x2d, rows, tm):
        merged = merge_branches(oz_a, oz_b, g_m, woa, wob, tm)
        return final_out(merged, x2d, gate[rows.start:rows.stop], wo, fg, tm)

    assert t_p % MOBA_BLOCK == 0
    xp = x_prompt.reshape(t_p, d_model)
    tm_p = 1024 if t_p % 1024 == 0 else 256
    nsa_f, nsa_b, moba_f, moba_b, z_p, gm_p, ga_p = project(
        xp, slice(0, 1), jnp.arange(t_p, dtype=jnp.int32), tm_p)
    n_cmp_p = (t_p - CMP_BLOCK) // CMP_STRIDE + 1
    n_sub_p = t_p // CMP_STRIDE
    assert n_cmp_p == n_sub_p - 1 and n_sub_p % 8 == 0
    sub_p = nsa_f[:, w_a:w_a + 2 * kv_a].reshape(n_sub_p, CMP_STRIDE, 4, HEAD_DIM)
    sub_p = sub_p.transpose(2, 0, 1, 3).reshape(1, 4, n_sub_p, half_w)
    kc_p = compress(sub_p, pe2, w1, w2)
    pool_p = _pool_matrix(n_sub_p, n_cmp_p)
    tk_p = max(t for t in (1024, 512, 256) if t_p % t == 0)
    cq_p = min(256, tk_p)
    assert (t_p - 1) // SEL_BLOCK < LANES
    win_p = jnp.pad(nsa_b[:, w_a + 4 * kv_a:], ((WINDOW, 0), (0, 0)))[None]
    nsa_b3 = nsa_b[None]
    oz_a = nsa_attention(
        nsa_b3, 0, ga_p[None], z_p[None], 0, kc_p,
        nsa_b3, (w_a + 2 * kv_a) // LANES, (w_a + 3 * kv_a) // LANES,
        win_p, 0, NSA_KV_HEADS, pool_p,
        cq=cq_p, tk=tk_p, qpos0=0, n_cmp=n_cmp_p, win_base=0)
    moba_f3 = moba_f[None]
    moba_b3 = moba_b[None]
    km_p = block_means(moba_f3, 1, w_b, blocks_per_step=8 if (t_p // MOBA_BLOCK) % 8 == 0 else t_p // MOBA_BLOCK)
    km_p = _pad_rows(km_p, LANES).astype(BF16)
    cqm_p = tk_p
    oz_b = moba_attention(moba_b3, 0, z_p[None], w_a // LANES, km_p, moba_b3, w_b // LANES,
                          2 * w_b // LANES, cq=cqm_p, tk=tk_p, qpos0=0)
    y_prompt = finish(oz_a[0], oz_b[0], gm_p, xp, slice(0, 1), 256).reshape(1, t_p, d_model)
    new_nsa_prompt = nsa_f[:, w_a:w_a + 4 * kv_a].reshape(1, 1, t_p, 4, NSA_KV_HEADS, HEAD_DIM)
    new_moba_prompt = moba_f[:, w_b:].reshape(1, 1, t_p, 2, MOBA_HEADS, HEAD_DIM)
    wlen = min(WINDOW, t_p)
    new_win_prompt = nsa_f[t_p - wlen:, w_a + 4 * kv_a:].reshape(1, 1, wlen, 2, NSA_KV_HEADS, HEAD_DIM)

    xs = x_sample.reshape(bs, d_model)
    pos_s = jnp.full((bs,), past, jnp.int32)
    nsa_fs, _, moba_fs, _, z_s, gm_s, ga_s = project(xs, slice(1, 1 + bs), pos_s, bs)
    l_s = past + 1
    n_cmp_s = (l_s - CMP_BLOCK) // CMP_STRIDE + 1
    n_sub_s = past // CMP_STRIDE
    assert n_cmp_s == n_sub_s - 1 and n_sub_s % 8 == 0
    assert past % MOBA_BLOCK == 0 and MOBA_BLOCK % page == 0 and page % SEL_BLOCK == 0
    assert wb == WINDOW and wb % 2 == 0 and past >= WINDOW
    n_phys = cache_nsa.shape[1]
    nsa_view = cache_nsa.reshape(n_phys, page, 4 * NSA_KV_HEADS, HEAD_DIM)
    moba_view = cache_moba.reshape(n_phys, page, 2 * MOBA_HEADS, HEAD_DIM)
    win_view = state_nsa_win.reshape(bs, wb // 2, 4 * NSA_KV_HEADS, HEAD_DIM)

    def head_rows(a2d):
        a = a2d.reshape(bs, -1, HEAD_DIM)
        return jnp.pad(a, ((0, 0), (0, HEAD_ROWS - a.shape[1]), (0, 0)))

    qa16 = head_rows(nsa_fs[:, :w_a])
    qb16 = head_rows(moba_fs[:, :w_b])
    za16 = head_rows(z_s[:, :w_a])
    zb16 = head_rows(z_s[:, w_a:])
    ga16 = ga_s.reshape(bs, NSA_KV_HEADS, LANES)[:, :, :3 * NSA_GROUP].reshape(bs, NSA_HEADS, 3)
    ga16 = jnp.pad(ga16, ((0, 0), (0, HEAD_ROWS - NSA_HEADS), (0, LANES - 3)))
    new_a16 = head_rows(nsa_fs[:, w_a + 2 * kv_a:])
    new_b16 = head_rows(moba_fs[:, w_b:])

    n_sel_blk = -(-(past // SEL_BLOCK + 1) // LANES) * LANES
    pool_s = _pool_matrix(n_sub_s, n_cmp_s, n_sel_blk).T
    ocmp16, imp = decode_cmp(nsa_view, page_table, qa16, pe2, w1, w2, pool_s, n_cmp=n_cmp_s, qpos=past)
    kmean_s = decode_kmean(moba_view, page_table,
                           blocks_per_step=8 if (past // MOBA_BLOCK) % 8 == 0 else 1)
    imp_t = imp[:, :NSA_KV_HEADS, :].transpose(2, 0, 1).reshape(n_sel_blk, bs * NSA_KV_HEADS)
    nidx, midx = decode_select(imp_t, kmean_s, qb16, qpos=past)
    oa16, ob16 = decode_attend(nsa_view, moba_view, page_table, nidx, midx,
                               [qa16, qb16, ga16, za16, zb16, ocmp16, new_a16, new_b16], win_view,
                               past=past, win_base=past - wb)
    oz_as = oa16[:, :NSA_HEADS].reshape(bs, w_a)
    oz_bs = ob16[:, :MOBA_HEADS].reshape(bs, w_b)
    y_sample = finish(oz_as, oz_bs, gm_s, xs, slice(1, 1 + bs), bs).reshape(bs, 1, d_model)
    new_nsa_sample = nsa_fs[:, w_a:w_a + 4 * kv_a].reshape(1, bs, 1, 4, NSA_KV_HEADS, HEAD_DIM)
    new_moba_sample = moba_fs[:, w_b:].reshape(1, bs, 1, 2, MOBA_HEADS, HEAD_DIM)
    new_win_sample = jnp.concatenate(
        [state_nsa_win[:, :, 1:],
         nsa_fs[:, w_a + 4 * kv_a:].reshape(1, bs, 1, 2, NSA_KV_HEADS, HEAD_DIM)], axis=2)

    return (y_prompt, y_sample, new_nsa_prompt, new_nsa_sample, new_moba_prompt, new_moba_sample,
            new_win_prompt, new_win_sample)
```

```python
import functools

import numpy as np
import jax
import jax.numpy as jnp
from jax import lax
from jax.experimental import pallas as pl
from jax.experimental.pallas import tpu as pltpu

F32 = jnp.float32
BF16 = jnp.bfloat16

HEAD_DIM = 128
NSA_HEADS = 8
NSA_KV_HEADS = 2
NSA_GROUP = NSA_HEADS // NSA_KV_HEADS
CMP_BLOCK = 32
CMP_STRIDE = 16
SEL_BLOCK = 64
N_SEL = 16
WINDOW = 512
PHI_HIDDEN = 2 * HEAD_DIM
MOBA_HEADS = 8
MOBA_BLOCK = 256
MOBA_TOPK = 3
ROPE_THETA = 10000.0
NORM_EPS = 1e-6

LANES = 128
MASK_BIAS = -(2.0 ** 100)
M_INIT = -1e30
LOG2_E = 1.4426950408889634
VMEM_LIMIT = 56 * 1024 * 1024

NT_DIMS = (((1,), (1,)), ((), ()))
TN_DIMS = (((0,), (0,)), ((), ()))


def _cparams(sem):
    return pltpu.CompilerParams(dimension_semantics=sem, vmem_limit_bytes=VMEM_LIMIT)


def _sigmoid(x):
    return 1.0 / (1.0 + jnp.exp(-x))


def _ada_kernel(c_ref, w_ref, b_ref, o_ref):
    c = c_ref[...]
    a = (c * _sigmoid(c)).astype(BF16)
    o_ref[...] = jnp.dot(a, w_ref[...].astype(BF16), preferred_element_type=F32) + b_ref[...]


def ada_mod(c, w, b, tn=768):
    m, k = c.shape
    n = w.shape[1]
    return pl.pallas_call(
        _ada_kernel,
        out_shape=jax.ShapeDtypeStruct((m, n), F32),
        grid=(n // tn,),
        in_specs=[pl.BlockSpec((m, k), lambda j: (0, 0)),
                  pl.BlockSpec((k, tn), lambda j: (0, j)),
                  pl.BlockSpec((1, tn), lambda j: (0, j))],
        out_specs=pl.BlockSpec((m, tn), lambda j: (0, j)),
        compiler_params=_cparams(("arbitrary",)),
        name="ada_mod",
    )(c, w, b)


def _h_kernel(x_ref, g_ref, sc_ref, sh_ref, o_ref):
    x = x_ref[...]
    ms = jnp.mean(x * x, axis=-1, keepdims=True)
    h = x * lax.rsqrt(ms + NORM_EPS) * g_ref[...]
    o_ref[...] = (h * (1.0 + sc_ref[...]) + sh_ref[...]).astype(BF16)


def norm_modulate(x, g, scale, shift, tm):
    m, d = x.shape
    per_row = scale.shape[0] != 1
    mod_spec = (pl.BlockSpec((tm, d), lambda i: (i, 0)) if per_row
                else pl.BlockSpec((1, d), lambda i: (0, 0)))
    return pl.pallas_call(
        _h_kernel,
        out_shape=jax.ShapeDtypeStruct((m, d), BF16),
        grid=(m // tm,),
        in_specs=[pl.BlockSpec((tm, d), lambda i: (i, 0)),
                  pl.BlockSpec((1, d), lambda i: (0, 0)),
                  mod_spec, mod_spec],
        out_specs=pl.BlockSpec((tm, d), lambda i: (i, 0)),
        compiler_params=_cparams(("arbitrary",)),
        name="norm_modulate",
    )(x, g, scale, shift)


def _proj_rope_kernel(h_ref, w_ref, cos_ref, sin_ref, of_ref, ob_ref, *, full_tiles, partial):
    acc = jnp.dot(h_ref[...], w_ref[...], preferred_element_type=F32)
    all_heads = pl.program_id(1) < full_tiles
    cos = cos_ref[...]
    sin = sin_ref[...]
    for t in range(acc.shape[1] // LANES):
        a = acc[:, t * LANES:(t + 1) * LANES]
        r = a * cos + pltpu.roll(a, HEAD_DIM // 2, axis=1) * sin
        out = r if t < partial else jnp.where(all_heads, r, a)
        of_ref[:, t * LANES:(t + 1) * LANES] = out
        ob_ref[:, t * LANES:(t + 1) * LANES] = out.astype(BF16)


def proj_rope(h, w, cols, cos2, sin2, tm, tn, full_tiles, partial):
    m, k = h.shape
    n = cols.stop - cols.start
    off = cols.start // tn
    assert cols.start % tn == 0 and n % tn == 0
    return pl.pallas_call(
        functools.partial(_proj_rope_kernel, full_tiles=full_tiles, partial=partial),
        out_shape=(jax.ShapeDtypeStruct((m, n), F32), jax.ShapeDtypeStruct((m, n), BF16)),
        grid=(m // tm, n // tn),
        in_specs=[pl.BlockSpec((tm, k), lambda i, j: (i, 0)),
                  pl.BlockSpec((k, tn), lambda i, j: (0, off + j)),
                  pl.BlockSpec((tm, LANES), lambda i, j: (i, 0)),
                  pl.BlockSpec((tm, LANES), lambda i, j: (i, 0))],
        out_specs=(pl.BlockSpec((tm, tn), lambda i, j: (i, j)),
                   pl.BlockSpec((tm, tn), lambda i, j: (i, j))),
        compiler_params=_cparams(("arbitrary", "arbitrary")),
        name="proj_rope",
    )(h, w, cos2, sin2)


def _proj_act_kernel(h_ref, w_ref, o_ref, *, act):
    acc = jnp.dot(h_ref[...], w_ref[...], preferred_element_type=F32)
    s = _sigmoid(acc)
    o_ref[...] = (acc * s if act == "silu" else s).astype(o_ref.dtype)


def proj_act(h, w, cols, act, tm, tn):
    m, k = h.shape
    n = cols.stop - cols.start
    off = cols.start // tn
    assert cols.start % tn == 0 and n % tn == 0
    return pl.pallas_call(
        functools.partial(_proj_act_kernel, act=act),
        out_shape=jax.ShapeDtypeStruct((m, n), F32),
        grid=(m // tm, n // tn),
        in_specs=[pl.BlockSpec((tm, k), lambda i, j: (i, 0)),
                  pl.BlockSpec((k, tn), lambda i, j: (0, off + j))],
        out_specs=pl.BlockSpec((tm, tn), lambda i, j: (i, j)),
        compiler_params=_cparams(("arbitrary", "arbitrary")),
        name="proj_" + act,
    )(h, w)


def _compress_kernel(s_ref, pe_ref, w1_ref, w2_ref, o_ref):
    s = s_ref[0, 0]
    half = s.shape[1]
    top = jnp.dot((s + pe_ref[0, 0:1]).astype(BF16), w1_ref[0, :half], preferred_element_type=F32)
    bot = jnp.dot((s + pe_ref[0, 1:2]).astype(BF16), w1_ref[0, half:], preferred_element_type=F32)
    hid = top + pltpu.roll(bot, s.shape[0] - 1, axis=0)
    hid = (hid * _sigmoid(hid)).astype(BF16)
    o_ref[0, 0] = jnp.dot(hid, w2_ref[0], preferred_element_type=F32).astype(BF16)


def compress(sub, pe2, w1, w2):
    b, four, n_sub, width = sub.shape
    return pl.pallas_call(
        _compress_kernel,
        out_shape=jax.ShapeDtypeStruct((b, four, n_sub, HEAD_DIM), BF16),
        grid=(four, b),
        in_specs=[pl.BlockSpec((1, 1, n_sub, width), lambda j, i: (i, j, 0, 0)),
                  pl.BlockSpec((1, 2, width), lambda j, i: (j // 2, 0, 0)),
                  pl.BlockSpec((1, 2 * width, PHI_HIDDEN), lambda j, i: (j // 2, 0, 0)),
                  pl.BlockSpec((1, PHI_HIDDEN, HEAD_DIM), lambda j, i: (j // 2, 0, 0))],
        out_specs=pl.BlockSpec((1, 1, n_sub, HEAD_DIM), lambda j, i: (i, j, 0, 0)),
        compiler_params=_cparams(("arbitrary", "arbitrary")),
        name="compress",
    )(sub, pe2, w1, w2)


def _kmean_kernel(k_ref, o_ref):
    k = k_ref[0]
    nb = k.shape[0] // MOBA_BLOCK
    o_ref[0] = jnp.mean(k.reshape(nb, MOBA_BLOCK, k.shape[1]), axis=1)


def block_means(rows, col_block, width, blocks_per_step=8):
    b, t, _ = rows.shape
    nb = t // MOBA_BLOCK
    return pl.pallas_call(
        _kmean_kernel,
        out_shape=jax.ShapeDtypeStruct((b, nb, width), F32),
        grid=(b, nb // blocks_per_step),
        in_specs=[pl.BlockSpec((1, blocks_per_step * MOBA_BLOCK, width),
                               lambda i, j: (i, j, col_block))],
        out_specs=pl.BlockSpec((1, blocks_per_step, width), lambda i, j: (i, j, 0)),
        compiler_params=_cparams(("arbitrary", "arbitrary")),
        name="block_means",
    )(rows)


def _masked_softmax_cols(s, valid, scale):
    s = jnp.where(valid, s, -jnp.inf)
    m = jnp.max(s, axis=0, keepdims=True)
    m = jnp.where(m > -jnp.inf, m, 0.0)
    p = jnp.exp2((s - m) * (scale * LOG2_E))
    return p / jnp.maximum(jnp.sum(p, axis=0, keepdims=True), 1e-30)


def _transpose(x):
    rows, cols = x.shape
    pr, pc = -rows % LANES, -cols % LANES
    if pr:
        x = jnp.concatenate([x, jnp.zeros((pr, cols), x.dtype)], axis=0)
    if pc:
        x = jnp.concatenate([x, jnp.zeros((rows + pr, pc), x.dtype)], axis=1)
    return x.T[:cols, :rows]


def _select_top(score, blk, n_iter, limit):
    sel = jnp.zeros(score.shape, jnp.bool_)
    s = score
    blk = blk.astype(F32)
    n_blk = float(score.shape[0])
    for it in range(n_iter):
        mx = jnp.max(s, axis=0, keepdims=True)
        idx = jnp.min(jnp.where(s == mx, blk, n_blk), axis=0, keepdims=True)
        pick = blk == idx
        ok = mx > -jnp.inf
        if limit is not None:
            ok = ok & (limit > it)
        sel = sel | (pick & ok)
        s = jnp.where(pick, -jnp.inf, s)
    return sel


def _block_sparse_flash(qp, k_ref, v_ref, tq_lane, cs, tk, blk_shift, scale):
    r = qp.shape[0]
    row = lax.broadcasted_iota(jnp.int32, (tk, LANES), 0)
    lane = lax.broadcasted_iota(jnp.int32, (tk, LANES), 1)
    rel_blk = lane - (row >> blk_shift)
    blocks_per_tile = tk >> blk_shift

    c = scale * LOG2_E
    def tile(i, carry, causal):
        m, l, acc = carry
        start = pl.multiple_of(i * tk, tk)
        k_t = k_ref[0, pl.ds(start, tk), :]
        v_t = v_ref[0, pl.ds(start, tk), :]
        onehot = jnp.where(rel_blk == i * blocks_per_tile, 1.0, 0.0).astype(BF16)
        kp = jnp.concatenate([k_t, onehot], axis=1)
        s = lax.dot_general(kp, qp, NT_DIMS, preferred_element_type=F32)
        if causal:
            kpos = start + lax.broadcasted_iota(jnp.int32, (tk, 1), 0)
            s = jnp.where(kpos <= tq_lane, s, M_INIT)
        m_new = jnp.maximum(m, jnp.max(s, axis=0, keepdims=True))
        alpha = jnp.exp2((m - m_new) * c)
        p = jnp.exp2((s - m_new) * c)
        l = alpha * l + jnp.sum(p, axis=0, keepdims=True)
        pv = lax.dot_general(v_t, p.astype(BF16), TN_DIMS, preferred_element_type=F32)
        return m_new, l, alpha * acc + pv

    init = (jnp.full((1, r), M_INIT, F32), jnp.zeros((1, r), F32), jnp.zeros((HEAD_DIM, r), F32))
    n_full = cs // tk
    carry = lax.fori_loop(0, n_full, lambda i, cr: tile(i, cr, False), init)
    _, l, acc = tile(n_full, carry, True)
    return acc / l


def _nsa_kernel(q_ref, ga_ref, z_ref, kc_ref, vc_ref, ks_ref, vs_ref, kw_ref, vw_ref, pool_ref,
                o_ref, *, cq, tk, qpos0, n_cmp, win_base):
    c = pl.program_id(2)
    r = NSA_GROUP * cq
    cs = qpos0 + c * cq
    scale = HEAD_DIM ** -0.5
    q = q_ref[0]
    qs = jnp.concatenate([q[:, h * LANES:(h + 1) * LANES] for h in range(NSA_GROUP)], axis=0)
    tq_lane = cs + (lax.broadcasted_iota(jnp.int32, (1, r), 1) & (cq - 1))

    kc = kc_ref[0, 0]
    nc_pad = kc.shape[0]
    n_id = lax.broadcasted_iota(jnp.int32, (nc_pad, 1), 0)
    sc = lax.dot_general(kc, qs, NT_DIMS, preferred_element_type=F32)
    cmp_end = jnp.where(n_id < n_cmp, n_id * CMP_STRIDE + (CMP_BLOCK - 1), jnp.iinfo(jnp.int32).max)
    pc = _masked_softmax_cols(sc, cmp_end <= tq_lane, scale)
    o_cmp = lax.dot_general(vc_ref[0, 0], pc.astype(BF16), TN_DIMS, preferred_element_type=F32)

    psum = pc[:, 0:cq]
    for h in range(1, NSA_GROUP):
        psum = psum + pc[:, h * cq:(h + 1) * cq]
    hi = psum.astype(BF16)
    r1 = psum - hi.astype(F32)
    mid = r1.astype(BF16)
    lo = (r1 - mid.astype(F32)).astype(BF16)
    imp = jnp.dot(pool_ref[...], jnp.concatenate([hi, mid, lo], axis=0), preferred_element_type=F32)

    blk = lax.broadcasted_iota(jnp.int32, (LANES, 1), 0)
    tq = cs + lax.broadcasted_iota(jnp.int32, (1, cq), 1)
    own = tq >> 6
    open_blk = blk * SEL_BLOCK <= tq
    forced = open_blk & ((blk == 0) | (blk == own) | (blk == own - 1))
    score = jnp.where(forced, jnp.inf, jnp.where(open_blk, imp, -jnp.inf))
    n_beyond = (own >= LANES).astype(jnp.int32) + (own - 1 >= LANES).astype(jnp.int32)
    sel = _select_top(score, blk, N_SEL, N_SEL - n_beyond)
    bias = _transpose(jnp.where(sel, 0.0, MASK_BIAS)).astype(BF16)
    qp = jnp.concatenate([qs, jnp.concatenate([bias] * NSA_GROUP, axis=0)], axis=1)
    o_slc = _block_sparse_flash(qp, ks_ref, vs_ref, tq_lane, cs, tk, 6, scale)

    tw = WINDOW + cq
    w_start = pl.multiple_of(c * cq, cq)
    k_w = kw_ref[0, pl.ds(w_start, tw), :]
    v_w = vw_ref[0, pl.ds(w_start, tw), :]
    wpos = (cs - WINDOW) + lax.broadcasted_iota(jnp.int32, (tw, 1), 0)
    sw = lax.dot_general(k_w, qs, NT_DIMS, preferred_element_type=F32)
    wpos = jnp.where(wpos >= win_base, wpos, jnp.iinfo(jnp.int32).min // 2)
    wvalid = (tq_lane - wpos).astype(jnp.uint32) < WINDOW
    pw = _masked_softmax_cols(sw, wvalid, scale)
    o_win = lax.dot_general(v_w, pw.astype(BF16), TN_DIMS, preferred_element_type=F32)

    g = _transpose(ga_ref[0])
    for h in range(NSA_GROUP):
        cols = slice(h * cq, (h + 1) * cq)
        o = (g[3 * h:3 * h + 1] * o_cmp[:, cols] + g[3 * h + 1:3 * h + 2] * o_slc[:, cols]
             + g[3 * h + 2:3 * h + 3] * o_win[:, cols])
        o_ref[0, :, h * LANES:(h + 1) * LANES] = (
            _transpose(o) * z_ref[0, :, h * LANES:(h + 1) * LANES]).astype(BF16)


def nsa_attention(q, q_col, gates, z, z_col, kc, ks, ks_col, vs_col, kw, kw_col, vw_col, pool,
                  *, cq, tk, qpos0, n_cmp, win_base):
    b, tq_len, _ = q.shape
    tkv = ks.shape[1]
    tw_len = kw.shape[1]
    nc_pad = kc.shape[2]
    gw = NSA_GROUP * LANES
    kern = functools.partial(_nsa_kernel, cq=cq, tk=tk, qpos0=qpos0, n_cmp=n_cmp, win_base=win_base)
    return pl.pallas_call(
        kern,
        out_shape=jax.ShapeDtypeStruct((b, tq_len, NSA_HEADS * HEAD_DIM), BF16),
        grid=(b, NSA_KV_HEADS, tq_len // cq),
        in_specs=[
            pl.BlockSpec((1, cq, gw), lambda i, g, c: (i, c, q_col + g)),
            pl.BlockSpec((1, cq, LANES), lambda i, g, c: (i, c, g)),
            pl.BlockSpec((1, cq, gw), lambda i, g, c: (i, c, z_col + g)),
            pl.BlockSpec((1, 1, nc_pad, HEAD_DIM), lambda i, g, c: (i, g, 0, 0)),
            pl.BlockSpec((1, 1, nc_pad, HEAD_DIM), lambda i, g, c: (i, 2 + g, 0, 0)),
            pl.BlockSpec((1, tkv, HEAD_DIM), lambda i, g, c: (i, 0, ks_col + g)),
            pl.BlockSpec((1, tkv, HEAD_DIM), lambda i, g, c: (i, 0, vs_col + g)),
            pl.BlockSpec((1, tw_len, HEAD_DIM), lambda i, g, c: (i, 0, kw_col + g)),
            pl.BlockSpec((1, tw_len, HEAD_DIM), lambda i, g, c: (i, 0, vw_col + g)),
            pl.BlockSpec(pool.shape, lambda i, g, c: (0, 0)),
        ],
        out_specs=pl.BlockSpec((1, cq, gw), lambda i, g, c: (i, c, g)),
        compiler_params=_cparams(("arbitrary", "arbitrary", "arbitrary")),
        name="nsa_attention",
    )(q, gates, z, kc, kc, ks, ks, kw, kw, pool)


def _moba_kernel(q_ref, z_ref, km_ref, k_ref, v_ref, o_ref, *, cq, tk, qpos0):
    c = pl.program_id(2)
    cs = qpos0 + c * cq
    scale = HEAD_DIM ** -0.5
    q = q_ref[0]
    tq = cs + lax.broadcasted_iota(jnp.int32, (1, cq), 1)
    blk = lax.broadcasted_iota(jnp.int32, (LANES, 1), 0)
    own = tq >> 8
    sc = lax.dot_general(km_ref[0], q, NT_DIMS, preferred_element_type=F32)
    score = jnp.where(blk < own, sc, -jnp.inf)
    sel = _select_top(score, blk, MOBA_TOPK, None) | (blk == own)
    bias = _transpose(jnp.where(sel, 0.0, MASK_BIAS)).astype(BF16)
    qp = jnp.concatenate([q, bias], axis=1)
    o = _block_sparse_flash(qp, k_ref, v_ref, tq, cs, tk, 8, scale)
    o_ref[0] = (_transpose(o) * z_ref[0]).astype(BF16)


def moba_attention(q, q_col, z, z_col, kmean, k, k_col, v_col, *, cq, tk, qpos0):
    b, tq_len, _ = q.shape
    tkv = k.shape[1]
    kern = functools.partial(_moba_kernel, cq=cq, tk=tk, qpos0=qpos0)
    return pl.pallas_call(
        kern,
        out_shape=jax.ShapeDtypeStruct((b, tq_len, MOBA_HEADS * HEAD_DIM), BF16),
        grid=(b, MOBA_HEADS, tq_len // cq),
        in_specs=[
            pl.BlockSpec((1, cq, LANES), lambda i, h, c: (i, c, q_col + h)),
            pl.BlockSpec((1, cq, LANES), lambda i, h, c: (i, c, z_col + h)),
            pl.BlockSpec((1, LANES, HEAD_DIM), lambda i, h, c: (i, 0, h)),
            pl.BlockSpec((1, tkv, HEAD_DIM), lambda i, h, c: (i, 0, k_col + h)),
            pl.BlockSpec((1, tkv, HEAD_DIM), lambda i, h, c: (i, 0, v_col + h)),
        ],
        out_specs=pl.BlockSpec((1, cq, LANES), lambda i, h, c: (i, c, h)),
        compiler_params=_cparams(("arbitrary", "arbitrary", "arbitrary")),
        name="moba_attention",
    )(q, z, kmean, k, k)


def _merge_kernel(oa_ref, ob_ref, ga_ref, gb_ref, wa_ref, wb_ref, o_ref):
    br_a = jnp.dot(oa_ref[...], wa_ref[...], preferred_element_type=F32)
    br_b = jnp.dot(ob_ref[...], wb_ref[...], preferred_element_type=F32)
    o_ref[...] = (ga_ref[...] * br_a + gb_ref[...] * br_b).astype(BF16)


def merge_branches(oz_a, oz_b, g_m, w_a, w_b, tm):
    m, ka = oz_a.shape
    d = w_a.shape[1]
    return pl.pallas_call(
        _merge_kernel,
        out_shape=jax.ShapeDtypeStruct((m, d), BF16),
        grid=(m // tm,),
        in_specs=[pl.BlockSpec((tm, ka), lambda i: (i, 0)),
                  pl.BlockSpec((tm, ka), lambda i: (i, 0)),
                  pl.BlockSpec((tm, d), lambda i: (i, 0)),
                  pl.BlockSpec((tm, d), lambda i: (i, 1)),
                  pl.BlockSpec((ka, d), lambda i: (0, 0)),
                  pl.BlockSpec((ka, d), lambda i: (0, 0))],
        out_specs=pl.BlockSpec((tm, d), lambda i: (i, 0)),
        compiler_params=_cparams(("arbitrary",)),
        name="merge_branches",
    )(oz_a, oz_b, g_m, g_m, w_a, w_b)


def _final_kernel(m_ref, x_ref, gate_ref, w_ref, g_ref, o_ref):
    y = x_ref[...] + gate_ref[...] * jnp.dot(m_ref[...], w_ref[...], preferred_element_type=F32)
    ms = jnp.mean(y * y, axis=-1, keepdims=True)
    o_ref[...] = y * lax.rsqrt(ms + NORM_EPS) * g_ref[...]


def final_out(merged, x, gate, w, g, tm):
    m, d = x.shape
    per_row = gate.shape[0] != 1
    gate_spec = (pl.BlockSpec((tm, d), lambda i: (i, 0)) if per_row
                 else pl.BlockSpec((1, d), lambda i: (0, 0)))
    return pl.pallas_call(
        _final_kernel,
        out_shape=jax.ShapeDtypeStruct((m, d), F32),
        grid=(m // tm,),
        in_specs=[pl.BlockSpec((tm, d), lambda i: (i, 0)),
                  pl.BlockSpec((tm, d), lambda i: (i, 0)),
                  gate_spec,
                  pl.BlockSpec((d, d), lambda i: (0, 0)),
                  pl.BlockSpec((1, d), lambda i: (0, 0))],
        out_specs=pl.BlockSpec((tm, d), lambda i: (i, 0)),
        compiler_params=_cparams(("arbitrary",)),
        name="final_out",
    )(merged, x, gate, w, g)


HEAD_ROWS = 16


def _softmax_rows_with_extra(s, valid, s_new):
    s = jnp.where(valid, s, -jnp.inf)
    m = jnp.maximum(jnp.max(s, axis=1, keepdims=True), s_new)
    p = jnp.exp(s - m)
    p_new = jnp.exp(s_new - m)
    l = jnp.sum(p, axis=1, keepdims=True) + p_new
    return p / l, p_new / l


def _bf16_round(x):
    return x.astype(BF16).astype(F32)


def _decode_cmp_kernel(pt_ref, cache_ref, q_ref, pe_ref, w1_ref, w2_ref, pool_ref, *rest,
                       n_pages, page, n_cmp, qpos, n_in):
    moba_pages = rest[:n_in]
    ocmp_ref, imp_ref, kmean_ref, bufk, bufv, xtop, xbot, kc_scr, vc_scr, sem = rest[n_in:]
    s = pl.program_id(0)
    j = pl.program_id(1)
    n_seq = pl.num_programs(0)
    last_j = pl.num_programs(1) - 1
    pages_per_block = MOBA_BLOCK // page
    for b in range(n_in // pages_per_block):
        tot = jnp.sum(moba_pages[b * pages_per_block][0], axis=0)
        for pp in range(1, pages_per_block):
            tot = tot + jnp.sum(moba_pages[b * pages_per_block + pp][0], axis=0)
        kmean_ref[0, b] = tot / MOBA_BLOCK
    n_sub = n_pages * page // CMP_STRIDE
    half = CMP_STRIDE * HEAD_DIM
    scale = HEAD_DIM ** -0.5

    def page_copies(seq, kv, buf, fn):
        def body(p, c):
            pid = pt_ref[seq, p]
            for g in range(NSA_KV_HEADS):
                fn(pltpu.make_async_copy(cache_ref.at[pid, :, kv * NSA_KV_HEADS + g, :],
                                         buf.at[g, pl.ds(p * page, page), :], sem.at[kv]), g)
            return c
        lax.fori_loop(0, n_pages, body, 0)

    def start(cp, lane):
        cp.start(priority=lane % 2)

    def wait(cp, lane):
        cp.wait()

    def compress_half(buf, kv):
        for r in range(CMP_STRIDE):
            x = jnp.concatenate([buf[g, pl.ds(r, n_sub, stride=CMP_STRIDE), :]
                                 for g in range(NSA_KV_HEADS)], axis=0)
            cols = slice(r * HEAD_DIM, (r + 1) * HEAD_DIM)
            xtop[:, cols] = (x + pe_ref[kv, 0:1, cols]).astype(BF16)
            xbot[:, cols] = (x + pe_ref[kv, 1:2, cols]).astype(BF16)
        top = jnp.dot(xtop[...], w1_ref[kv, :half], preferred_element_type=F32)
        bot = jnp.dot(xbot[...], w1_ref[kv, half:], preferred_element_type=F32)
        outs = []
        for g in range(NSA_KV_HEADS):
            rows = slice(g * n_sub, (g + 1) * n_sub)
            hid = top[rows] + pltpu.roll(bot[rows], n_sub - 1, axis=0)
            hid = (hid * _sigmoid(hid)).astype(BF16)
            outs.append(jnp.dot(hid, w2_ref[kv], preferred_element_type=F32).astype(BF16))
        return outs

    @pl.when(j == 0)
    def _():
        @pl.when(s == 0)
        def _():
            page_copies(0, 0, bufk, start)
            page_copies(0, 1, bufv, start)

        page_copies(s, 0, bufk, wait)
        kc = compress_half(bufk, 0)
        for g in range(NSA_KV_HEADS):
            kc_scr[g] = kc[g]

        @pl.when(s + 1 < n_seq)
        def _():
            page_copies(s + 1, 0, bufk, start)

    @pl.when(j == jnp.minimum(1, last_j))
    def _():
        page_copies(s, 1, bufv, wait)
        vc = compress_half(bufv, 1)
        for g in range(NSA_KV_HEADS):
            vc_scr[g] = vc[g]

        @pl.when(s + 1 < n_seq)
        def _():
            page_copies(s + 1, 1, bufv, start)

    @pl.when(j == jnp.minimum(2, last_j))
    def _():
        q = q_ref[0].astype(BF16)
        row_g = lax.broadcasted_iota(jnp.int32, (HEAD_ROWS, 1), 0) >> 2
        sc = [lax.dot_general(q, kc_scr[g], NT_DIMS, preferred_element_type=F32)
              for g in range(NSA_KV_HEADS)]
        sc = jnp.where(row_g == 0, sc[0], sc[1]) * scale
        n_id = lax.broadcasted_iota(jnp.int32, (1, n_sub), 1)
        valid = ((n_id * CMP_STRIDE + (CMP_BLOCK - 1)) <= qpos) & (n_id < n_cmp)
        sc = jnp.where(valid, sc, -jnp.inf)
        m = jnp.max(sc, axis=1, keepdims=True)
        m = jnp.where(m > -jnp.inf, m, 0.0)
        p = jnp.exp(sc - m)
        pc = p / jnp.maximum(jnp.sum(p, axis=1, keepdims=True), 1e-30)
        pcb = pc.astype(BF16)
        o = [jnp.dot(pcb, vc_scr[g], preferred_element_type=F32) for g in range(NSA_KV_HEADS)]
        ocmp_ref[0] = jnp.where(row_g == 0, o[0], o[1])
        psum = jnp.concatenate(
            [jnp.sum(pc[g * NSA_GROUP:(g + 1) * NSA_GROUP], axis=0, keepdims=True)
             for g in range(NSA_KV_HEADS)]
            + [jnp.zeros((HEAD_ROWS - NSA_KV_HEADS, n_sub), F32)], axis=0)
        hi = psum.astype(BF16)
        r1 = psum - hi.astype(F32)
        mid = r1.astype(BF16)
        lo = (r1 - mid.astype(F32)).astype(BF16)
        imp_ref[0] = jnp.dot(jnp.concatenate([hi, mid, lo], axis=1), pool_ref[...],
                             preferred_element_type=F32)


def decode_cmp(cache_view, moba_view, page_table, q16, pe2, w1, w2, pool_nat, *, n_cmp, qpos, blocks_per_step):
    bs, n_pages = page_table.shape
    page = cache_view.shape[1]
    past = n_pages * page
    n_sub = past // CMP_STRIDE
    nb = pool_nat.shape[1]
    half = CMP_STRIDE * HEAD_DIM
    ppb = MOBA_BLOCK // page
    n_blocks = n_pages // ppb
    n_in = blocks_per_step * ppb

    def whole(a):
        return pl.BlockSpec(a.shape, lambda i, j, pt: (0,) * a.ndim)

    key_pages = [pl.BlockSpec((1, page, MOBA_HEADS, HEAD_DIM),
                              lambda i, j, pt, k=k: (pt[i, j * n_in + k], 0, 0, 0)) for k in range(n_in)]
    grid_spec = pltpu.PrefetchScalarGridSpec(
        num_scalar_prefetch=1,
        grid=(bs, n_blocks // blocks_per_step),
        in_specs=[pl.BlockSpec(memory_space=pl.ANY),
                  pl.BlockSpec((1, HEAD_ROWS, HEAD_DIM), lambda i, j, pt: (i, 0, 0)),
                  whole(pe2), whole(w1), whole(w2), whole(pool_nat)] + key_pages,
        out_specs=(pl.BlockSpec((1, HEAD_ROWS, HEAD_DIM), lambda i, j, pt: (i, 0, 0)),
                   pl.BlockSpec((1, HEAD_ROWS, nb), lambda i, j, pt: (i, 0, 0)),
                   pl.BlockSpec((1, blocks_per_step, MOBA_HEADS, HEAD_DIM), lambda i, j, pt: (i, j, 0, 0))),
        scratch_shapes=[pltpu.VMEM((NSA_KV_HEADS, past, HEAD_DIM), F32),
                        pltpu.VMEM((NSA_KV_HEADS, past, HEAD_DIM), F32),
                        pltpu.VMEM((NSA_KV_HEADS * n_sub, half), BF16),
                        pltpu.VMEM((NSA_KV_HEADS * n_sub, half), BF16),
                        pltpu.VMEM((NSA_KV_HEADS, n_sub, HEAD_DIM), BF16),
                        pltpu.VMEM((NSA_KV_HEADS, n_sub, HEAD_DIM), BF16),
                        pltpu.SemaphoreType.DMA((2,))],
    )
    return pl.pallas_call(
        functools.partial(_decode_cmp_kernel, n_pages=n_pages, page=page, n_cmp=n_cmp, qpos=qpos, n_in=n_in),
        out_shape=(jax.ShapeDtypeStruct((bs, HEAD_ROWS, HEAD_DIM), F32),
                   jax.ShapeDtypeStruct((bs, HEAD_ROWS, nb), F32),
                   jax.ShapeDtypeStruct((bs, n_blocks, MOBA_HEADS, HEAD_DIM), F32)),
        grid_spec=grid_spec,
        compiler_params=_cparams(("arbitrary", "arbitrary")),
        name="decode_cmp",
    )(page_table, cache_view, q16, pe2, w1, w2, pool_nat, *([moba_view] * n_in))


def _top_indices(score, blk, n_iter):
    rows = []
    s = score
    blk = blk.astype(F32)
    n_blk = float(score.shape[0])
    for _ in range(n_iter):
        mx = jnp.max(s, axis=0, keepdims=True)
        idx = jnp.min(jnp.where(s == mx, blk, n_blk), axis=0, keepdims=True)
        rows.append(jnp.where(mx > -jnp.inf, idx, -1.0))
        s = jnp.where(blk == idx, -jnp.inf, s)
    return rows


def _decode_select_kernel(imp_ref, km_ref, q_ref, nidx_ref, midx_ref, *, qpos):
    nbk = imp_ref.shape[0]
    blk = lax.broadcasted_iota(jnp.int32, (nbk, 1), 0)
    own = qpos // SEL_BLOCK
    open_blk = blk * SEL_BLOCK <= qpos
    forced = open_blk & ((blk == 0) | (blk == own) | (blk == own - 1))
    score = jnp.where(forced, jnp.inf, jnp.where(open_blk, imp_ref[...], -jnp.inf))
    nidx_ref[...] = jnp.concatenate(_top_indices(score, blk, N_SEL), axis=0).astype(jnp.int32)

    n_seq, n_blocks = km_ref.shape[0], km_ref.shape[1]
    n_col = n_seq * MOBA_HEADS
    lane = lax.broadcasted_iota(jnp.int32, (1, n_col), 1)

    def body(i, sct):
        q = _bf16_round(q_ref[i])
        for h in range(MOBA_HEADS):
            prod = _bf16_round(km_ref[i, :, h, :]) * q[h:h + 1]
            col = jnp.sum(prod, axis=1, keepdims=True)
            sct = jnp.where(lane == i * MOBA_HEADS + h, col, sct)
        return sct

    sct = lax.fori_loop(0, n_seq, body, jnp.zeros((n_blocks, n_col), F32))
    blk_m = lax.broadcasted_iota(jnp.int32, (n_blocks, 1), 0)
    score_m = jnp.where(blk_m < qpos // MOBA_BLOCK, sct, -jnp.inf)
    rows = _top_indices(score_m, blk_m, MOBA_TOPK)
    rows = rows + [jnp.full((1, n_col), -1.0, F32)] * (midx_ref.shape[0] - MOBA_TOPK)
    midx_ref[...] = jnp.concatenate(rows, axis=0).astype(jnp.int32)


def decode_select(imp_t, kmean, q16, *, qpos):
    n_col_a = imp_t.shape[1]
    n_col_b = kmean.shape[0] * MOBA_HEADS

    def whole(a):
        return pl.BlockSpec(a.shape, lambda i: (0,) * a.ndim)

    return pl.pallas_call(
        functools.partial(_decode_select_kernel, qpos=qpos),
        out_shape=(jax.ShapeDtypeStruct((N_SEL, n_col_a), jnp.int32),
                   jax.ShapeDtypeStruct((8, n_col_b), jnp.int32)),
        grid=(1,),
        in_specs=[whole(imp_t), whole(kmean), whole(q16)],
        out_specs=(pl.BlockSpec((N_SEL, n_col_a), lambda i: (0, 0)),
                   pl.BlockSpec((8, n_col_b), lambda i: (0, 0))),
        compiler_params=_cparams(("arbitrary",)),
        name="decode_select",
    )(imp_t, kmean, q16)


def _decode_attend_kernel(pt_ref, nidx_ref, midx_ref, nsa_ref, moba_ref,
                          qa_ref, qb_ref, ga_ref, za_ref, zb_ref, ocmp_ref, newa_ref, newb_ref, win_ref,
                          oa_ref, ob_ref, ks, vs, km, vm, sem, *, page, past, win_base):
    s = pl.program_id(0)
    n_seq = pl.num_programs(0)
    slot = lax.rem(s, 2)
    scale = HEAD_DIM ** -0.5
    blocks_per_page = page // SEL_BLOCK
    pages_per_block = MOBA_BLOCK // page
    n_cache_blk = past // SEL_BLOCK

    def nsa_copies(seq, sl, fn):
        for g in range(NSA_KV_HEADS):
            def body(i, c, g=g):
                blk = jnp.clip(nidx_ref[i, seq * NSA_KV_HEADS + g], 0, n_cache_blk - 1)
                pid = pt_ref[seq, blk // blocks_per_page]
                rows = pl.ds((blk % blocks_per_page) * SEL_BLOCK, SEL_BLOCK)
                dst = pl.ds(i * SEL_BLOCK, SEL_BLOCK)
                fn(pltpu.make_async_copy(nsa_ref.at[pid, rows, 2 * NSA_KV_HEADS + g, :],
                                         ks.at[sl, g, dst, :], sem.at[sl, 0]), 0)
                fn(pltpu.make_async_copy(nsa_ref.at[pid, rows, 3 * NSA_KV_HEADS + g, :],
                                         vs.at[sl, g, dst, :], sem.at[sl, 0]), 1)
                return c
            lax.fori_loop(0, N_SEL, body, 0)

    def moba_copies(seq, sl, fn):
        for h in range(MOBA_HEADS):
            def body(i, c, h=h):
                blk = jnp.maximum(midx_ref[i, seq * MOBA_HEADS + h], 0)
                for pp in range(pages_per_block):
                    pid = pt_ref[seq, blk * pages_per_block + pp]
                    dst = pl.ds((i * pages_per_block + pp) * page, page)
                    fn(pltpu.make_async_copy(moba_ref.at[pid, :, h, :], km.at[sl, h, dst, :], sem.at[sl, 1]), 0)
                    fn(pltpu.make_async_copy(moba_ref.at[pid, :, MOBA_HEADS + h, :],
                                             vm.at[sl, h, dst, :], sem.at[sl, 1]), 1)
                return c
            lax.fori_loop(0, MOBA_TOPK, body, 0)

    def start(cp, lane):
        cp.start(priority=lane % 2)

    def wait(cp, lane):
        cp.wait()

    @pl.when(s == 0)
    def _():
        nsa_copies(0, 0, start)
        moba_copies(0, 0, start)

    @pl.when(s + 1 < n_seq)
    def _():
        nsa_copies(s + 1, 1 - slot, start)
        moba_copies(s + 1, 1 - slot, start)

    nsa_copies(s, slot, wait)
    moba_copies(s, slot, wait)

    row = lax.broadcasted_iota(jnp.int32, (HEAD_ROWS, 1), 0)
    row_g = row >> 2
    qa = qa_ref[0].astype(BF16)
    qa_f = qa.astype(F32)
    new_a = _bf16_round(newa_ref[0])

    n_keys = N_SEL * SEL_BLOCK
    lane_blk = lax.broadcasted_iota(jnp.int32, (1, n_keys), 1) >> (SEL_BLOCK.bit_length() - 1)
    o_slc = []
    for g in range(NSA_KV_HEADS):
        k = ks[slot, g].astype(BF16)
        v = vs[slot, g].astype(BF16)
        sc = lax.dot_general(qa, k, NT_DIMS, preferred_element_type=F32) * scale
        okv = jnp.zeros((1, n_keys), jnp.int32)
        for i in range(N_SEL):
            b = nidx_ref[i, s * NSA_KV_HEADS + g]
            ok = ((b >= 0) & (b < n_cache_blk)).astype(jnp.int32)
            okv = jnp.where(lane_blk == i, ok, okv)
        s_new = jnp.sum(qa_f * new_a[g:g + 1], axis=1, keepdims=True) * scale
        pn, pn_new = _softmax_rows_with_extra(sc, okv > 0, s_new)
        o_slc.append(jnp.dot(pn.astype(BF16), v, preferred_element_type=F32)
                     + _bf16_round(pn_new) * new_a[NSA_KV_HEADS + g:NSA_KV_HEADS + g + 1])
    o_slc = jnp.where(row_g == 0, o_slc[0], o_slc[1])

    half_w = win_ref.shape[1]
    pos_e = win_base + 2 * lax.broadcasted_iota(jnp.int32, (1, half_w), 1)
    o_win = []
    for g in range(NSA_KV_HEADS):
        parts = []
        for parity in range(2):
            k = win_ref[0, :, 4 * parity + g, :].astype(BF16)
            sc = lax.dot_general(qa, k, NT_DIMS, preferred_element_type=F32) * scale
            pos = pos_e + parity
            ok = (pos <= past) & (pos > past - WINDOW) & (pos >= win_base)
            parts.append(jnp.where(ok, sc, -jnp.inf))
        s_new = jnp.sum(qa_f * new_a[4 + g:5 + g], axis=1, keepdims=True) * scale
        sc = jnp.concatenate(parts, axis=1)
        pn, pn_new = _softmax_rows_with_extra(sc, sc > -jnp.inf, s_new)
        o = _bf16_round(pn_new) * new_a[6 + g:7 + g]
        for parity in range(2):
            v = win_ref[0, :, 4 * parity + 2 + g, :].astype(BF16)
            o = o + jnp.dot(pn[:, parity * half_w:(parity + 1) * half_w].astype(BF16), v,
                            preferred_element_type=F32)
        o_win.append(o)
    o_win = jnp.where(row_g == 0, o_win[0], o_win[1])

    gates = ga_ref[0]
    o_a = gates[:, 0:1] * ocmp_ref[0] + gates[:, 1:2] * o_slc + gates[:, 2:3] * o_win
    oa_ref[0] = (o_a * za_ref[0]).astype(BF16)

    qb = qb_ref[0].astype(BF16)
    qb_f = qb.astype(F32)
    new_b = _bf16_round(newb_ref[0])
    n_keys_b = MOBA_TOPK * MOBA_BLOCK
    lane_blk_b = lax.broadcasted_iota(jnp.int32, (1, n_keys_b), 1) >> (MOBA_BLOCK.bit_length() - 1)
    o_b = jnp.zeros((HEAD_ROWS, HEAD_DIM), F32)
    for h in range(MOBA_HEADS):
        k = km[slot, h].astype(BF16)
        v = vm[slot, h].astype(BF16)
        sc = lax.dot_general(qb, k, NT_DIMS, preferred_element_type=F32) * scale
        okv = jnp.zeros((1, n_keys_b), jnp.int32)
        for i in range(MOBA_TOPK):
            ok = (midx_ref[i, s * MOBA_HEADS + h] >= 0).astype(jnp.int32)
            okv = jnp.where(lane_blk_b == i, ok, okv)
        s_new = jnp.sum(qb_f * new_b[h:h + 1], axis=1, keepdims=True) * scale
        pn, pn_new = _softmax_rows_with_extra(sc, okv > 0, s_new)
        o = (jnp.dot(pn.astype(BF16), v, preferred_element_type=F32)
             + _bf16_round(pn_new) * new_b[MOBA_HEADS + h:MOBA_HEADS + h + 1])
        o_b = jnp.where(row == h, o, o_b)
    ob_ref[0] = (o_b * zb_ref[0]).astype(BF16)


def decode_attend(nsa_view, moba_view, page_table, nidx, midx, per_seq, win_view, *, past, win_base):
    bs = page_table.shape[0]
    page = nsa_view.shape[1]
    row_spec = pl.BlockSpec((1, HEAD_ROWS, HEAD_DIM), lambda i, *_: (i, 0, 0))
    grid_spec = pltpu.PrefetchScalarGridSpec(
        num_scalar_prefetch=3,
        grid=(bs,),
        in_specs=[pl.BlockSpec(memory_space=pl.ANY), pl.BlockSpec(memory_space=pl.ANY)]
        + [row_spec] * len(per_seq)
        + [pl.BlockSpec((1,) + win_view.shape[1:], lambda i, *_: (i, 0, 0, 0))],
        out_specs=(row_spec, row_spec),
        scratch_shapes=[pltpu.VMEM((2, NSA_KV_HEADS, N_SEL * SEL_BLOCK, HEAD_DIM), F32),
                        pltpu.VMEM((2, NSA_KV_HEADS, N_SEL * SEL_BLOCK, HEAD_DIM), F32),
                        pltpu.VMEM((2, MOBA_HEADS, MOBA_TOPK * MOBA_BLOCK, HEAD_DIM), F32),
                        pltpu.VMEM((2, MOBA_HEADS, MOBA_TOPK * MOBA_BLOCK, HEAD_DIM), F32),
                        pltpu.SemaphoreType.DMA((2, 2))],
    )
    return pl.pallas_call(
        functools.partial(_decode_attend_kernel, page=page, past=past, win_base=win_base),
        out_shape=(jax.ShapeDtypeStruct((bs, HEAD_ROWS, HEAD_DIM), BF16),
                   jax.ShapeDtypeStruct((bs, HEAD_ROWS, HEAD_DIM), BF16)),
        grid_spec=grid_spec,
        compiler_params=_cparams(("arbitrary",)),
        name="decode_attend",
    )(page_table, nidx, midx, nsa_view, moba_view, *per_seq, win_view)


def _rope_tables(pos):
    half = HEAD_DIM // 2
    inv = ROPE_THETA ** (-(jnp.arange(half, dtype=F32) / half))
    ang = pos.astype(F32)[:, None] * inv[None, :]
    cos, sin = jnp.cos(ang), jnp.sin(ang)
    return jnp.concatenate([cos, cos], axis=1), jnp.concatenate([-sin, sin], axis=1)


def _pool_matrix(nc_pad, n_cmp, n_blk=LANES):
    ratio = SEL_BLOCK // CMP_STRIDE
    n_sub = CMP_BLOCK // CMP_STRIDE
    j = np.arange(n_blk)[:, None]
    n = np.arange(nc_pad)[None, :]
    pool = ((n >= ratio * j - (n_sub - 1)) & (n <= ratio * j + ratio - 1) & (n < n_cmp)).astype(np.float32)
    return jnp.asarray(np.concatenate([pool, pool, pool], axis=1), dtype=BF16)


def _pad_rows(a, rows):
    return jnp.pad(a, ((0, 0), (0, rows - a.shape[1]), (0, 0)))


def kernel(x_prompt, x_sample, cache_nsa, cache_moba, state_nsa_win, page_table, c_prompt, c_sample,
           w_ada, b_ada, norm_g, w_in, pe_k, w_phi_k1, w_phi_k2, pe_v, w_phi_v1, w_phi_v2,
           w_out_a, w_out_b, w_out, final_g):
    depth = w_in.shape[0]
    assert depth == 1, "single-layer step"
    d_model = x_prompt.shape[-1]
    bp, t_p, _ = x_prompt.shape
    bs, t_s, _ = x_sample.shape
    assert bp == 1 and t_s == 1
    page = cache_nsa.shape[2]
    n_pages = page_table.shape[1]
    past = n_pages * page
    wb = state_nsa_win.shape[2]
    w_a = NSA_HEADS * HEAD_DIM
    w_b = MOBA_HEADS * HEAD_DIM
    kv_a = NSA_KV_HEADS * HEAD_DIM
    s0 = w_a + 6 * kv_a
    s1 = s0 + 3 * NSA_HEADS
    s2 = s1 + w_a
    s3 = s2 + 3 * w_b
    s4 = s3 + w_b

    wi = w_in[0]
    w_ga = wi[:, s0:s1].reshape(d_model, NSA_KV_HEADS, 3 * NSA_GROUP)
    w_ga = jnp.pad(w_ga, ((0, 0), (0, 0), (0, LANES - 3 * NSA_GROUP))).reshape(d_model, -1)
    w_all = jnp.concatenate([wi[:, :s0], wi[:, s2:s3], wi[:, s1:s2], wi[:, s3:s4], wi[:, s4:], w_ga],
                            axis=1).astype(BF16)
    c_nsa = slice(0, s0)
    c_moba = slice(c_nsa.stop, c_nsa.stop + 3 * w_b)
    c_z = slice(c_moba.stop, c_moba.stop + w_a + w_b)
    c_gm = slice(c_z.stop, c_z.stop + 2 * d_model)
    c_ga = slice(c_gm.stop, c_gm.stop + NSA_KV_HEADS * LANES)
    w1 = jnp.stack([w_phi_k1[0], w_phi_v1[0]]).astype(BF16)
    w2 = jnp.stack([w_phi_k2[0], w_phi_v2[0]]).astype(BF16)
    half_w = CMP_STRIDE * HEAD_DIM
    pe2 = jnp.stack([pe_k[0].reshape(2, half_w), pe_v[0].reshape(2, half_w)])
    woa = w_out_a[0].astype(BF16)
    wob = w_out_b[0].astype(BF16)
    wo = w_out[0].astype(BF16)
    fg = final_g.reshape(1, d_model)
    ng = norm_g[0].reshape(1, d_model)

    c_all = jnp.concatenate([c_prompt, c_sample], axis=0)
    n_c = c_all.shape[0]
    c_all = jnp.pad(c_all, ((0, -n_c % 8), (0, 0)))
    mod = ada_mod(c_all, w_ada[0], b_ada[0].reshape(1, -1))
    shift, scale, gate = mod[:, :d_model], mod[:, d_model:2 * d_model], mod[:, 2 * d_model:]

    def project(x2d, rows, pos, tm):
        sl = slice(rows.start, rows.stop)
        h = norm_modulate(x2d, ng, scale[sl], shift[sl], tm)
        cos2, sin2 = _rope_tables(pos)
        nsa_f, nsa_b = proj_rope(h, w_all, c_nsa, cos2, sin2, tm, 512, w_a // 512, 2)
        moba_f, moba_b = proj_rope(h, w_all, c_moba, cos2, sin2, tm, 512, 2 * w_b // 512, 0)
        z = proj_act(h, w_all, c_z, "silu", tm, 512)
        g_m = proj_act(h, w_all, c_gm, "sigmoid", tm, 512)
        g_a = proj_act(h, w_all, c_ga, "sigmoid", tm, NSA_KV_HEADS * LANES)
        return nsa_f, nsa_b, moba_f, moba_b, z, g_m, g_a

    def finish(oz_a, oz_b, g_m, x2d, rows, tm):
        merged = merge_branches(oz_a, oz_b, g_m, woa, wob, tm)
        return final_out(merged, x2d, gate[rows.start:rows.stop], wo, fg, tm)

    assert t_p % MOBA_BLOCK == 0
    xp = x_prompt.reshape(t_p, d_model)
    tm_p = 1024 if t_p % 1024 == 0 else 256
    nsa_f, nsa_b, moba_f, moba_b, z_p, gm_p, ga_p = project(
        xp, slice(0, 1), jnp.arange(t_p, dtype=jnp.int32), tm_p)
    n_cmp_p = (t_p - CMP_BLOCK) // CMP_STRIDE + 1
    n_sub_p = t_p // CMP_STRIDE
    assert n_cmp_p == n_sub_p - 1 and n_sub_p % 8 == 0
    sub_p = nsa_f[:, w_a:w_a + 2 * kv_a].reshape(n_sub_p, CMP_STRIDE, 4, HEAD_DIM)
    sub_p = sub_p.transpose(2, 0, 1, 3).reshape(1, 4, n_sub_p, half_w)
    kc_p = compress(sub_p, pe2, w1, w2)
    pool_p = _pool_matrix(n_sub_p, n_cmp_p)
    tk_p = max(t for t in (1024, 512, 256) if t_p % t == 0)
    cq_p = min(256, tk_p)
    assert (t_p - 1) // SEL_BLOCK < LANES
    win_p = jnp.pad(nsa_b[:, w_a + 4 * kv_a:], ((WINDOW, 0), (0, 0)))[None]
    nsa_b3 = nsa_b[None]
    oz_a = nsa_attention(
        nsa_b3, 0, ga_p[None], z_p[None], 0, kc_p,
        nsa_b3, (w_a + 2 * kv_a) // LANES, (w_a + 3 * kv_a) // LANES,
        win_p, 0, NSA_KV_HEADS, pool_p,
        cq=cq_p, tk=tk_p, qpos0=0, n_cmp=n_cmp_p, win_base=0)
    moba_f3 = moba_f[None]
    moba_b3 = moba_b[None]
    km_p = block_means(moba_f3, 1, w_b, blocks_per_step=8 if (t_p // MOBA_BLOCK) % 8 == 0 else t_p // MOBA_BLOCK)
    km_p = _pad_rows(km_p, LANES).astype(BF16)
    cqm_p = tk_p
    oz_b = moba_attention(moba_b3, 0, z_p[None], w_a // LANES, km_p, moba_b3, w_b // LANES,
                          2 * w_b // LANES, cq=cqm_p, tk=tk_p, qpos0=0)
    y_prompt = finish(oz_a[0], oz_b[0], gm_p, xp, slice(0, 1), 256).reshape(1, t_p, d_model)
    new_nsa_prompt = nsa_f[:, w_a:w_a + 4 * kv_a].reshape(1, 1, t_p, 4, NSA_KV_HEADS, HEAD_DIM)
    new_moba_prompt = moba_f[:, w_b:].reshape(1, 1, t_p, 2, MOBA_HEADS, HEAD_DIM)
    wlen = min(WINDOW, t_p)
    new_win_prompt = nsa_f[t_p - wlen:, w_a + 4 * kv_a:].reshape(1, 1, wlen, 2, NSA_KV_HEADS, HEAD_DIM)

    xs = x_sample.reshape(bs, d_model)
    pos_s = jnp.full((bs,), past, jnp.int32)
    nsa_fs, _, moba_fs, _, z_s, gm_s, ga_s = project(xs, slice(1, 1 + bs), pos_s, bs)
    l_s = past + 1
    n_cmp_s = (l_s - CMP_BLOCK) // CMP_STRIDE + 1
    n_sub_s = past // CMP_STRIDE
    assert n_cmp_s == n_sub_s - 1 and n_sub_s % 8 == 0
    assert past % MOBA_BLOCK == 0 and MOBA_BLOCK % page == 0 and page % SEL_BLOCK == 0
    assert wb == WINDOW and wb % 2 == 0 and past >= WINDOW
    n_phys = cache_nsa.shape[1]
    nsa_view = cache_nsa.reshape(n_phys, page, 4 * NSA_KV_HEADS, HEAD_DIM)
    moba_view = cache_moba.reshape(n_phys, page, 2 * MOBA_HEADS, HEAD_DIM)
    win_view = state_nsa_win.reshape(bs, wb // 2, 4 * NSA_KV_HEADS, HEAD_DIM)

    def head_rows(a2d):
        a = a2d.reshape(bs, -1, HEAD_DIM)
        return jnp.pad(a, ((0, 0), (0, HEAD_ROWS - a.shape[1]), (0, 0)))

    qa16 = head_rows(nsa_fs[:, :w_a])
    qb16 = head_rows(moba_fs[:, :w_b])
    za16 = head_rows(z_s[:, :w_a])
    zb16 = head_rows(z_s[:, w_a:])
    ga16 = ga_s.reshape(bs, NSA_KV_HEADS, LANES)[:, :, :3 * NSA_GROUP].reshape(bs, NSA_HEADS, 3)
    ga16 = jnp.pad(ga16, ((0, 0), (0, HEAD_ROWS - NSA_HEADS), (0, LANES - 3)))
    new_a16 = head_rows(nsa_fs[:, w_a + 2 * kv_a:])
    new_b16 = head_rows(moba_fs[:, w_b:])

    n_sel_blk = -(-(past // SEL_BLOCK + 1) // LANES) * LANES
    pool_s = _pool_matrix(n_sub_s, n_cmp_s, n_sel_blk).T
    ocmp16, imp, kmean_s = decode_cmp(nsa_view, moba_view, page_table, qa16, pe2, w1, w2, pool_s,
                                      n_cmp=n_cmp_s, qpos=past,
                                      blocks_per_step=4 if (past // MOBA_BLOCK) % 4 == 0 else 1)
    imp_t = imp[:, :NSA_KV_HEADS, :].transpose(2, 0, 1).reshape(n_sel_blk, bs * NSA_KV_HEADS)
    nidx, midx = decode_select(imp_t, kmean_s, qb16, qpos=past)
    oa16, ob16 = decode_attend(nsa_view, moba_view, page_table, nidx, midx,
                               [qa16, qb16, ga16, za16, zb16, ocmp16, new_a16, new_b16], win_view,
                               past=past, win_base=past - wb)
    oz_as = oa16[:, :NSA_HEADS].reshape(bs, w_a)
    oz_bs = ob16[:, :MOBA_HEADS].reshape(bs, w_b)
    y_sample = finish(oz_as, oz_bs, gm_s, xs, slice(1, 1 + bs), bs).reshape(bs, 1, d_model)
    new_nsa_sample = nsa_fs[:, w_a:w_a + 4 * kv_a].reshape(1, bs, 1, 4, NSA_KV_HEADS, HEAD_DIM)
    new_moba_sample = moba_fs[:, w_b:].reshape(1, bs, 1, 2, MOBA_HEADS, HEAD_DIM)
    new_win_sample = jnp.concatenate(
        [state_nsa_win[:, :, 1:],
         nsa_fs[:, w_a + 4 * kv_a:].reshape(1, bs, 1, 2, NSA_KV_HEADS, HEAD_DIM)], axis=2)

    return (y_prompt, y_sample, new_nsa_prompt, new_nsa_sample, new_moba_prompt, new_moba_sample,
            new_win_prompt, new_win_sample)
```

```python
import functools

import numpy as np
import jax
import jax.numpy as jnp
from jax import lax
from jax.experimental import pallas as pl
from jax.experimental.pallas import tpu as pltpu

F32 = jnp.float32
BF16 = jnp.bfloat16

HEAD_DIM = 128
NSA_HEADS = 8
NSA_KV_HEADS = 2
NSA_GROUP = NSA_HEADS // NSA_KV_HEADS
CMP_BLOCK = 32
CMP_STRIDE = 16
SEL_BLOCK = 64
N_SEL = 16
WINDOW = 512
PHI_HIDDEN = 2 * HEAD_DIM
MOBA_HEADS = 8
MOBA_BLOCK = 256
MOBA_TOPK = 3
ROPE_THETA = 10000.0
NORM_EPS = 1e-6

LANES = 128
MASK_BIAS = -(2.0 ** 100)
M_INIT = -1e30
LOG2_E = 1.4426950408889634
VMEM_LIMIT = 56 * 1024 * 1024

NT_DIMS = (((1,), (1,)), ((), ()))
TN_DIMS = (((0,), (0,)), ((), ()))


def _cparams(sem):
    return pltpu.CompilerParams(dimension_semantics=sem, vmem_limit_bytes=VMEM_LIMIT)


def _sigmoid(x):
    return 1.0 / (1.0 + jnp.exp(-x))


def _ada_kernel(c_ref, w_ref, b_ref, o_ref):
    c = c_ref[...]
    a = (c * _sigmoid(c)).astype(BF16)
    o_ref[...] = jnp.dot(a, w_ref[...].astype(BF16), preferred_element_type=F32) + b_ref[...]


def ada_mod(c, w, b, tn=768):
    m, k = c.shape
    n = w.shape[1]
    return pl.pallas_call(
        _ada_kernel,
        out_shape=jax.ShapeDtypeStruct((m, n), F32),
        grid=(n // tn,),
        in_specs=[pl.BlockSpec((m, k), lambda j: (0, 0)),
                  pl.BlockSpec((k, tn), lambda j: (0, j)),
                  pl.BlockSpec((1, tn), lambda j: (0, j))],
        out_specs=pl.BlockSpec((m, tn), lambda j: (0, j)),
        compiler_params=_cparams(("arbitrary",)),
        name="ada_mod",
    )(c, w, b)


def _h_kernel(x_ref, g_ref, sc_ref, sh_ref, o_ref):
    x = x_ref[...]
    ms = jnp.mean(x * x, axis=-1, keepdims=True)
    h = x * lax.rsqrt(ms + NORM_EPS) * g_ref[...]
    o_ref[...] = (h * (1.0 + sc_ref[...]) + sh_ref[...]).astype(BF16)


def norm_modulate(x, g, scale, shift, tm):
    m, d = x.shape
    per_row = scale.shape[0] != 1
    mod_spec = (pl.BlockSpec((tm, d), lambda i: (i, 0)) if per_row
                else pl.BlockSpec((1, d), lambda i: (0, 0)))
    return pl.pallas_call(
        _h_kernel,
        out_shape=jax.ShapeDtypeStruct((m, d), BF16),
        grid=(m // tm,),
        in_specs=[pl.BlockSpec((tm, d), lambda i: (i, 0)),
                  pl.BlockSpec((1, d), lambda i: (0, 0)),
                  mod_spec, mod_spec],
        out_specs=pl.BlockSpec((tm, d), lambda i: (i, 0)),
        compiler_params=_cparams(("arbitrary",)),
        name="norm_modulate",
    )(x, g, scale, shift)


def _proj_rope_kernel(h_ref, w_ref, cos_ref, sin_ref, of_ref, ob_ref, *, full_tiles, partial):
    acc = jnp.dot(h_ref[...], w_ref[...], preferred_element_type=F32)
    all_heads = pl.program_id(1) < full_tiles
    cos = cos_ref[...]
    sin = sin_ref[...]
    for t in range(acc.shape[1] // LANES):
        a = acc[:, t * LANES:(t + 1) * LANES]
        r = a * cos + pltpu.roll(a, HEAD_DIM // 2, axis=1) * sin
        out = r if t < partial else jnp.where(all_heads, r, a)
        of_ref[:, t * LANES:(t + 1) * LANES] = out
        ob_ref[:, t * LANES:(t + 1) * LANES] = out.astype(BF16)


def proj_rope(h, w, cols, cos2, sin2, tm, tn, full_tiles, partial):
    m, k = h.shape
    n = cols.stop - cols.start
    off = cols.start // tn
    assert cols.start % tn == 0 and n % tn == 0
    return pl.pallas_call(
        functools.partial(_proj_rope_kernel, full_tiles=full_tiles, partial=partial),
        out_shape=(jax.ShapeDtypeStruct((m, n), F32), jax.ShapeDtypeStruct((m, n), BF16)),
        grid=(m // tm, n // tn),
        in_specs=[pl.BlockSpec((tm, k), lambda i, j: (i, 0)),
                  pl.BlockSpec((k, tn), lambda i, j: (0, off + j)),
                  pl.BlockSpec((tm, LANES), lambda i, j: (i, 0)),
                  pl.BlockSpec((tm, LANES), lambda i, j: (i, 0))],
        out_specs=(pl.BlockSpec((tm, tn), lambda i, j: (i, j)),
                   pl.BlockSpec((tm, tn), lambda i, j: (i, j))),
        compiler_params=_cparams(("arbitrary", "arbitrary")),
        name="proj_rope",
    )(h, w, cos2, sin2)


def _proj_act_kernel(h_ref, w_ref, o_ref, *, act):
    acc = jnp.dot(h_ref[...], w_ref[...], preferred_element_type=F32)
    s = _sigmoid(acc)
    o_ref[...] = (acc * s if act == "silu" else s).astype(o_ref.dtype)


def proj_act(h, w, cols, act, tm, tn):
    m, k = h.shape
    n = cols.stop - cols.start
    off = cols.start // tn
    assert cols.start % tn == 0 and n % tn == 0
    return pl.pallas_call(
        functools.partial(_proj_act_kernel, act=act),
        out_shape=jax.ShapeDtypeStruct((m, n), F32),
        grid=(m // tm, n // tn),
        in_specs=[pl.BlockSpec((tm, k), lambda i, j: (i, 0)),
                  pl.BlockSpec((k, tn), lambda i, j: (0, off + j))],
        out_specs=pl.BlockSpec((tm, tn), lambda i, j: (i, j)),
        compiler_params=_cparams(("arbitrary", "arbitrary")),
        name="proj_" + act,
    )(h, w)


def _compress_kernel(s_ref, pe_ref, w1_ref, w2_ref, o_ref):
    s = s_ref[0, 0]
    half = s.shape[1]
    top = jnp.dot((s + pe_ref[0, 0:1]).astype(BF16), w1_ref[0, :half], preferred_element_type=F32)
    bot = jnp.dot((s + pe_ref[0, 1:2]).astype(BF16), w1_ref[0, half:], preferred_element_type=F32)
    hid = top + pltpu.roll(bot, s.shape[0] - 1, axis=0)
    hid = (hid * _sigmoid(hid)).astype(BF16)
    o_ref[0, 0] = jnp.dot(hid, w2_ref[0], preferred_element_type=F32).astype(BF16)


def compress(sub, pe2, w1, w2):
    b, four, n_sub, width = sub.shape
    return pl.pallas_call(
        _compress_kernel,
        out_shape=jax.ShapeDtypeStruct((b, four, n_sub, HEAD_DIM), BF16),
        grid=(four, b),
        in_specs=[pl.BlockSpec((1, 1, n_sub, width), lambda j, i: (i, j, 0, 0)),
                  pl.BlockSpec((1, 2, width), lambda j, i: (j // 2, 0, 0)),
                  pl.BlockSpec((1, 2 * width, PHI_HIDDEN), lambda j, i: (j // 2, 0, 0)),
                  pl.BlockSpec((1, PHI_HIDDEN, HEAD_DIM), lambda j, i: (j // 2, 0, 0))],
        out_specs=pl.BlockSpec((1, 1, n_sub, HEAD_DIM), lambda j, i: (i, j, 0, 0)),
        compiler_params=_cparams(("arbitrary", "arbitrary")),
        name="compress",
    )(sub, pe2, w1, w2)


def _kmean_kernel(k_ref, o_ref):
    k = k_ref[0]
    nb = k.shape[0] // MOBA_BLOCK
    o_ref[0] = jnp.mean(k.reshape(nb, MOBA_BLOCK, k.shape[1]), axis=1)


def block_means(rows, col_block, width, blocks_per_step=8):
    b, t, _ = rows.shape
    nb = t // MOBA_BLOCK
    return pl.pallas_call(
        _kmean_kernel,
        out_shape=jax.ShapeDtypeStruct((b, nb, width), F32),
        grid=(b, nb // blocks_per_step),
        in_specs=[pl.BlockSpec((1, blocks_per_step * MOBA_BLOCK, width),
                               lambda i, j: (i, j, col_block))],
        out_specs=pl.BlockSpec((1, blocks_per_step, width), lambda i, j: (i, j, 0)),
        compiler_params=_cparams(("arbitrary", "arbitrary")),
        name="block_means",
    )(rows)


def _masked_softmax_cols(s, valid, scale):
    s = jnp.where(valid, s, -jnp.inf)
    m = jnp.max(s, axis=0, keepdims=True)
    m = jnp.where(m > -jnp.inf, m, 0.0)
    p = jnp.exp2((s - m) * (scale * LOG2_E))
    return p / jnp.maximum(jnp.sum(p, axis=0, keepdims=True), 1e-30)


def _transpose(x):
    rows, cols = x.shape
    pr, pc = -rows % LANES, -cols % LANES
    if pr:
        x = jnp.concatenate([x, jnp.zeros((pr, cols), x.dtype)], axis=0)
    if pc:
        x = jnp.concatenate([x, jnp.zeros((rows + pr, pc), x.dtype)], axis=1)
    return x.T[:cols, :rows]


def _select_top(score, blk, n_iter, limit):
    sel = jnp.zeros(score.shape, jnp.bool_)
    s = score
    blk = blk.astype(F32)
    n_blk = float(score.shape[0])
    for it in range(n_iter):
        mx = jnp.max(s, axis=0, keepdims=True)
        idx = jnp.min(jnp.where(s == mx, blk, n_blk), axis=0, keepdims=True)
        pick = blk == idx
        ok = mx > -jnp.inf
        if limit is not None:
            ok = ok & (limit > it)
        sel = sel | (pick & ok)
        s = jnp.where(pick, -jnp.inf, s)
    return sel


def _block_sparse_flash(qp, k_ref, v_ref, s_a, s_b, tq_lane, cs, tk, blk_shift, scale):
    r = qp.shape[0]
    row = lax.broadcasted_iota(jnp.int32, (tk, LANES), 0)
    lane = lax.broadcasted_iota(jnp.int32, (tk, LANES), 1)
    rel_blk = lane - (row >> blk_shift)
    blocks_per_tile = tk >> blk_shift

    c = scale * LOG2_E
    n_full = cs // tk

    def scores_into(i, dst):
        i = jnp.minimum(i, n_full)
        start = pl.multiple_of(i * tk, tk)
        k_t = k_ref[0, pl.ds(start, tk), :]
        onehot = jnp.where(rel_blk == i * blocks_per_tile, 1.0, 0.0).astype(BF16)
        kp = jnp.concatenate([k_t, onehot], axis=1)
        dst[...] = lax.dot_general(kp, qp, NT_DIMS, preferred_element_type=F32)

    def consume(i, carry, src, causal):
        m, l, acc = carry
        start = pl.multiple_of(i * tk, tk)
        v_t = v_ref[0, pl.ds(start, tk), :]
        s = src[...]
        if causal:
            kpos = start + lax.broadcasted_iota(jnp.int32, (tk, 1), 0)
            s = jnp.where(kpos <= tq_lane, s, M_INIT)
        m_new = jnp.maximum(m, jnp.max(s, axis=0, keepdims=True))
        alpha = jnp.exp2((m - m_new) * c)
        p = jnp.exp2((s - m_new) * c)
        l = alpha * l + jnp.sum(p, axis=0, keepdims=True)
        pv = lax.dot_general(v_t, p.astype(BF16), TN_DIMS, preferred_element_type=F32)
        return m_new, l, alpha * acc + pv

    def pair(jp, carry):
        scores_into(2 * jp + 1, s_b)
        carry = consume(2 * jp, carry, s_a, False)
        scores_into(2 * jp + 2, s_a)
        return consume(2 * jp + 1, carry, s_b, False)

    def odd_tail(carry):
        scores_into(n_full, s_b)
        carry = consume(n_full - 1, carry, s_a, False)
        return consume(n_full, carry, s_b, True)

    init = (jnp.full((1, r), M_INIT, F32), jnp.zeros((1, r), F32), jnp.zeros((HEAD_DIM, r), F32))
    scores_into(0, s_a)
    carry = lax.fori_loop(0, n_full // 2, pair, init)
    _, l, acc = lax.cond(n_full % 2 == 1, odd_tail, lambda cr: consume(n_full, cr, s_a, True), carry)
    return acc / l


def _nsa_kernel(q_ref, ga_ref, z_ref, kc_ref, vc_ref, ks_ref, vs_ref, kw_ref, vw_ref, pool_ref,
                o_ref, s_a, s_b, *, cq, tk, qpos0, n_cmp, win_base):
    c = pl.program_id(2)
    r = NSA_GROUP * cq
    cs = qpos0 + c * cq
    scale = HEAD_DIM ** -0.5
    q = q_ref[0]
    qs = jnp.concatenate([q[:, h * LANES:(h + 1) * LANES] for h in range(NSA_GROUP)], axis=0)
    tq_lane = cs + (lax.broadcasted_iota(jnp.int32, (1, r), 1) & (cq - 1))

    kc = kc_ref[0, 0]
    nc_pad = kc.shape[0]
    n_id = lax.broadcasted_iota(jnp.int32, (nc_pad, 1), 0)
    sc = lax.dot_general(kc, qs, NT_DIMS, preferred_element_type=F32)
    cmp_end = jnp.where(n_id < n_cmp, n_id * CMP_STRIDE + (CMP_BLOCK - 1), jnp.iinfo(jnp.int32).max)
    pc = _masked_softmax_cols(sc, cmp_end <= tq_lane, scale)
    o_cmp = lax.dot_general(vc_ref[0, 0], pc.astype(BF16), TN_DIMS, preferred_element_type=F32)

    psum = pc[:, 0:cq]
    for h in range(1, NSA_GROUP):
        psum = psum + pc[:, h * cq:(h + 1) * cq]
    hi = psum.astype(BF16)
    r1 = psum - hi.astype(F32)
    mid = r1.astype(BF16)
    lo = (r1 - mid.astype(F32)).astype(BF16)
    imp = jnp.dot(pool_ref[...], jnp.concatenate([hi, mid, lo], axis=0), preferred_element_type=F32)

    blk = lax.broadcasted_iota(jnp.int32, (LANES, 1), 0)
    tq = cs + lax.broadcasted_iota(jnp.int32, (1, cq), 1)
    own = tq >> 6
    open_blk = blk * SEL_BLOCK <= tq
    forced = open_blk & ((blk == 0) | (blk == own) | (blk == own - 1))
    score = jnp.where(forced, jnp.inf, jnp.where(open_blk, imp, -jnp.inf))
    n_beyond = (own >= LANES).astype(jnp.int32) + (own - 1 >= LANES).astype(jnp.int32)
    sel = _select_top(score, blk, N_SEL, N_SEL - n_beyond)
    bias = _transpose(jnp.where(sel, 0.0, MASK_BIAS)).astype(BF16)
    qp = jnp.concatenate([qs, jnp.concatenate([bias] * NSA_GROUP, axis=0)], axis=1)
    o_slc = _block_sparse_flash(qp, ks_ref, vs_ref, s_a, s_b, tq_lane, cs, tk, 6, scale)

    tw = WINDOW + cq
    w_start = pl.multiple_of(c * cq, cq)
    k_w = kw_ref[0, pl.ds(w_start, tw), :]
    v_w = vw_ref[0, pl.ds(w_start, tw), :]
    wpos = (cs - WINDOW) + lax.broadcasted_iota(jnp.int32, (tw, 1), 0)
    sw = lax.dot_general(k_w, qs, NT_DIMS, preferred_element_type=F32)
    wpos = jnp.where(wpos >= win_base, wpos, jnp.iinfo(jnp.int32).min // 2)
    wvalid = (tq_lane - wpos).astype(jnp.uint32) < WINDOW
    pw = _masked_softmax_cols(sw, wvalid, scale)
    o_win = lax.dot_general(v_w, pw.astype(BF16), TN_DIMS, preferred_element_type=F32)

    g = _transpose(ga_ref[0])
    for h in range(NSA_GROUP):
        cols = slice(h * cq, (h + 1) * cq)
        o = (g[3 * h:3 * h + 1] * o_cmp[:, cols] + g[3 * h + 1:3 * h + 2] * o_slc[:, cols]
             + g[3 * h + 2:3 * h + 3] * o_win[:, cols])
        o_ref[0, :, h * LANES:(h + 1) * LANES] = (
            _transpose(o) * z_ref[0, :, h * LANES:(h + 1) * LANES]).astype(BF16)


def nsa_attention(q, q_col, gates, z, z_col, kc, ks, ks_col, vs_col, kw, kw_col, vw_col, pool,
                  *, cq, tk, qpos0, n_cmp, win_base):
    b, tq_len, _ = q.shape
    tkv = ks.shape[1]
    tw_len = kw.shape[1]
    nc_pad = kc.shape[2]
    gw = NSA_GROUP * LANES
    kern = functools.partial(_nsa_kernel, cq=cq, tk=tk, qpos0=qpos0, n_cmp=n_cmp, win_base=win_base)
    return pl.pallas_call(
        kern,
        out_shape=jax.ShapeDtypeStruct((b, tq_len, NSA_HEADS * HEAD_DIM), BF16),
        grid=(b, NSA_KV_HEADS, tq_len // cq),
        in_specs=[
            pl.BlockSpec((1, cq, gw), lambda i, g, c: (i, c, q_col + g)),
            pl.BlockSpec((1, cq, LANES), lambda i, g, c: (i, c, g)),
            pl.BlockSpec((1, cq, gw), lambda i, g, c: (i, c, z_col + g)),
            pl.BlockSpec((1, 1, nc_pad, HEAD_DIM), lambda i, g, c: (i, g, 0, 0)),
            pl.BlockSpec((1, 1, nc_pad, HEAD_DIM), lambda i, g, c: (i, 2 + g, 0, 0)),
            pl.BlockSpec((1, tkv, HEAD_DIM), lambda i, g, c: (i, 0, ks_col + g)),
            pl.BlockSpec((1, tkv, HEAD_DIM), lambda i, g, c: (i, 0, vs_col + g)),
            pl.BlockSpec((1, tw_len, HEAD_DIM), lambda i, g, c: (i, 0, kw_col + g)),
            pl.BlockSpec((1, tw_len, HEAD_DIM), lambda i, g, c: (i, 0, vw_col + g)),
            pl.BlockSpec(pool.shape, lambda i, g, c: (0, 0)),
        ],
        out_specs=pl.BlockSpec((1, cq, gw), lambda i, g, c: (i, c, g)),
        scratch_shapes=[pltpu.VMEM((tk, NSA_GROUP * cq), F32), pltpu.VMEM((tk, NSA_GROUP * cq), F32)],
        compiler_params=_cparams(("arbitrary", "arbitrary", "arbitrary")),
        name="nsa_attention",
    )(q, gates, z, kc, kc, ks, ks, kw, kw, pool)


def _moba_kernel(q_ref, z_ref, km_ref, k_ref, v_ref, o_ref, s_a, s_b, *, cq, tk, qpos0):
    c = pl.program_id(2)
    cs = qpos0 + c * cq
    scale = HEAD_DIM ** -0.5
    q = q_ref[0]
    tq = cs + lax.broadcasted_iota(jnp.int32, (1, cq), 1)
    blk = lax.broadcasted_iota(jnp.int32, (LANES, 1), 0)
    own = tq >> 8
    sc = lax.dot_general(km_ref[0], q, NT_DIMS, preferred_element_type=F32)
    score = jnp.where(blk < own, sc, -jnp.inf)
    sel = _select_top(score, blk, MOBA_TOPK, None) | (blk == own)
    bias = _transpose(jnp.where(sel, 0.0, MASK_BIAS)).astype(BF16)
    qp = jnp.concatenate([q, bias], axis=1)
    o = _block_sparse_flash(qp, k_ref, v_ref, s_a, s_b, tq, cs, tk, 8, scale)
    o_ref[0] = (_transpose(o) * z_ref[0]).astype(BF16)


def moba_attention(q, q_col, z, z_col, kmean, k, k_col, v_col, *, cq, tk, qpos0):
    b, tq_len, _ = q.shape
    tkv = k.shape[1]
    kern = functools.partial(_moba_kernel, cq=cq, tk=tk, qpos0=qpos0)
    return pl.pallas_call(
        kern,
        out_shape=jax.ShapeDtypeStruct((b, tq_len, MOBA_HEADS * HEAD_DIM), BF16),
        grid=(b, MOBA_HEADS, tq_len // cq),
        in_specs=[
            pl.BlockSpec((1, cq, LANES), lambda i, h, c: (i, c, q_col + h)),
            pl.BlockSpec((1, cq, LANES), lambda i, h, c: (i, c, z_col + h)),
            pl.BlockSpec((1, LANES, HEAD_DIM), lambda i, h, c: (i, 0, h)),
            pl.BlockSpec((1, tkv, HEAD_DIM), lambda i, h, c: (i, 0, k_col + h)),
            pl.BlockSpec((1, tkv, HEAD_DIM), lambda i, h, c: (i, 0, v_col + h)),
        ],
        out_specs=pl.BlockSpec((1, cq, LANES), lambda i, h, c: (i, c, h)),
        scratch_shapes=[pltpu.VMEM((tk, cq), F32), pltpu.VMEM((tk, cq), F32)],
        compiler_params=_cparams(("arbitrary", "arbitrary", "arbitrary")),
        name="moba_attention",
    )(q, z, kmean, k, k)


def _merge_kernel(oa_ref, ob_ref, ga_ref, gb_ref, wa_ref, wb_ref, o_ref):
    br_a = jnp.dot(oa_ref[...], wa_ref[...], preferred_element_type=F32)
    br_b = jnp.dot(ob_ref[...], wb_ref[...], preferred_element_type=F32)
    o_ref[...] = (ga_ref[...] * br_a + gb_ref[...] * br_b).astype(BF16)


def merge_branches(oz_a, oz_b, g_m, w_a, w_b, tm):
    m, ka = oz_a.shape
    d = w_a.shape[1]
    return pl.pallas_call(
        _merge_kernel,
        out_shape=jax.ShapeDtypeStruct((m, d), BF16),
        grid=(m // tm,),
        in_specs=[pl.BlockSpec((tm, ka), lambda i: (i, 0)),
                  pl.BlockSpec((tm, ka), lambda i: (i, 0)),
                  pl.BlockSpec((tm, d), lambda i: (i, 0)),
                  pl.BlockSpec((tm, d), lambda i: (i, 1)),
                  pl.BlockSpec((ka, d), lambda i: (0, 0)),
                  pl.BlockSpec((ka, d), lambda i: (0, 0))],
        out_specs=pl.BlockSpec((tm, d), lambda i: (i, 0)),
        compiler_params=_cparams(("arbitrary",)),
        name="merge_branches",
    )(oz_a, oz_b, g_m, g_m, w_a, w_b)


def _final_kernel(m_ref, x_ref, gate_ref, w_ref, g_ref, o_ref):
    y = x_ref[...] + gate_ref[...] * jnp.dot(m_ref[...], w_ref[...], preferred_element_type=F32)
    ms = jnp.mean(y * y, axis=-1, keepdims=True)
    o_ref[...] = y * lax.rsqrt(ms + NORM_EPS) * g_ref[...]


def final_out(merged, x, gate, w, g, tm):
    m, d = x.shape
    per_row = gate.shape[0] != 1
    gate_spec = (pl.BlockSpec((tm, d), lambda i: (i, 0)) if per_row
                 else pl.BlockSpec((1, d), lambda i: (0, 0)))
    return pl.pallas_call(
        _final_kernel,
        out_shape=jax.ShapeDtypeStruct((m, d), F32),
        grid=(m // tm,),
        in_specs=[pl.BlockSpec((tm, d), lambda i: (i, 0)),
                  pl.BlockSpec((tm, d), lambda i: (i, 0)),
                  gate_spec,
                  pl.BlockSpec((d, d), lambda i: (0, 0)),
                  pl.BlockSpec((1, d), lambda i: (0, 0))],
        out_specs=pl.BlockSpec((tm, d), lambda i: (i, 0)),
        compiler_params=_cparams(("arbitrary",)),
        name="final_out",
    )(merged, x, gate, w, g)


HEAD_ROWS = 16


def _softmax_rows_with_extra(s, valid, s_new):
    s = jnp.where(valid, s, -jnp.inf)
    m = jnp.maximum(jnp.max(s, axis=1, keepdims=True), s_new)
    p = jnp.exp(s - m)
    p_new = jnp.exp(s_new - m)
    l = jnp.sum(p, axis=1, keepdims=True) + p_new
    return p / l, p_new / l


def _bf16_round(x):
    return x.astype(BF16).astype(F32)


def _decode_cmp_kernel(pt_ref, cache_ref, q_ref, pe_ref, w1_ref, w2_ref, pool_ref, *rest,
                       n_pages, page, n_cmp, qpos, n_in):
    moba_pages = rest[:n_in]
    ocmp_ref, imp_ref, kmean_ref, bufk, bufv, xtop, xbot, kc_scr, vc_scr, sem = rest[n_in:]
    s = pl.program_id(0)
    j = pl.program_id(1)
    n_seq = pl.num_programs(0)
    last_j = pl.num_programs(1) - 1
    pages_per_block = MOBA_BLOCK // page
    for b in range(n_in // pages_per_block):
        tot = jnp.sum(moba_pages[b * pages_per_block][0], axis=0)
        for pp in range(1, pages_per_block):
            tot = tot + jnp.sum(moba_pages[b * pages_per_block + pp][0], axis=0)
        kmean_ref[0, b] = tot / MOBA_BLOCK
    n_sub = n_pages * page // CMP_STRIDE
    half = CMP_STRIDE * HEAD_DIM
    scale = HEAD_DIM ** -0.5

    def page_copies(seq, kv, buf, fn):
        def body(p, c):
            pid = pt_ref[seq, p]
            for g in range(NSA_KV_HEADS):
                fn(pltpu.make_async_copy(cache_ref.at[pid, :, kv * NSA_KV_HEADS + g, :],
                                         buf.at[g, pl.ds(p * page, page), :], sem.at[kv]), g)
            return c
        lax.fori_loop(0, n_pages, body, 0)

    def start(cp, lane):
        cp.start(priority=lane % 2)

    def wait(cp, lane):
        cp.wait()

    def compress_half(buf, kv):
        for r in range(CMP_STRIDE):
            x = jnp.concatenate([buf[g, pl.ds(r, n_sub, stride=CMP_STRIDE), :]
                                 for g in range(NSA_KV_HEADS)], axis=0)
            cols = slice(r * HEAD_DIM, (r + 1) * HEAD_DIM)
            xtop[:, cols] = (x + pe_ref[kv, 0:1, cols]).astype(BF16)
            xbot[:, cols] = (x + pe_ref[kv, 1:2, cols]).astype(BF16)
        top = jnp.dot(xtop[...], w1_ref[kv, :half], preferred_element_type=F32)
        bot = jnp.dot(xbot[...], w1_ref[kv, half:], preferred_element_type=F32)
        outs = []
        for g in range(NSA_KV_HEADS):
            rows = slice(g * n_sub, (g + 1) * n_sub)
            hid = top[rows] + pltpu.roll(bot[rows], n_sub - 1, axis=0)
            hid = (hid * _sigmoid(hid)).astype(BF16)
            outs.append(jnp.dot(hid, w2_ref[kv], preferred_element_type=F32).astype(BF16))
        return outs

    @pl.when(j == 0)
    def _():
        @pl.when(s == 0)
        def _():
            page_copies(0, 0, bufk, start)
            page_copies(0, 1, bufv, start)

        page_copies(s, 0, bufk, wait)
        kc = compress_half(bufk, 0)
        for g in range(NSA_KV_HEADS):
            kc_scr[g] = kc[g]

        @pl.when(s + 1 < n_seq)
        def _():
            page_copies(s + 1, 0, bufk, start)

    @pl.when(j == jnp.minimum(1, last_j))
    def _():
        page_copies(s, 1, bufv, wait)
        vc = compress_half(bufv, 1)
        for g in range(NSA_KV_HEADS):
            vc_scr[g] = vc[g]

        @pl.when(s + 1 < n_seq)
        def _():
            page_copies(s + 1, 1, bufv, start)

    @pl.when(j == jnp.minimum(2, last_j))
    def _():
        q = q_ref[0].astype(BF16)
        row_g = lax.broadcasted_iota(jnp.int32, (HEAD_ROWS, 1), 0) >> 2
        sc = [lax.dot_general(q, kc_scr[g], NT_DIMS, preferred_element_type=F32)
              for g in range(NSA_KV_HEADS)]
        sc = jnp.where(row_g == 0, sc[0], sc[1]) * scale
        n_id = lax.broadcasted_iota(jnp.int32, (1, n_sub), 1)
        valid = ((n_id * CMP_STRIDE + (CMP_BLOCK - 1)) <= qpos) & (n_id < n_cmp)
        sc = jnp.where(valid, sc, -jnp.inf)
        m = jnp.max(sc, axis=1, keepdims=True)
        m = jnp.where(m > -jnp.inf, m, 0.0)
        p = jnp.exp(sc - m)
        pc = p / jnp.maximum(jnp.sum(p, axis=1, keepdims=True), 1e-30)
        pcb = pc.astype(BF16)
        o = [jnp.dot(pcb, vc_scr[g], preferred_element_type=F32) for g in range(NSA_KV_HEADS)]
        ocmp_ref[0] = jnp.where(row_g == 0, o[0], o[1])
        psum = jnp.concatenate(
            [jnp.sum(pc[g * NSA_GROUP:(g + 1) * NSA_GROUP], axis=0, keepdims=True)
             for g in range(NSA_KV_HEADS)]
            + [jnp.zeros((HEAD_ROWS - NSA_KV_HEADS, n_sub), F32)], axis=0)
        hi = psum.astype(BF16)
        r1 = psum - hi.astype(F32)
        mid = r1.astype(BF16)
        lo = (r1 - mid.astype(F32)).astype(BF16)
        imp_ref[0] = jnp.dot(jnp.concatenate([hi, mid, lo], axis=1), pool_ref[...],
                             preferred_element_type=F32)


def decode_cmp(cache_view, moba_view, page_table, q16, pe2, w1, w2, pool_nat, *, n_cmp, qpos, blocks_per_step):
    bs, n_pages = page_table.shape
    page = cache_view.shape[1]
    past = n_pages * page
    n_sub = past // CMP_STRIDE
    nb = pool_nat.shape[1]
    half = CMP_STRIDE * HEAD_DIM
    ppb = MOBA_BLOCK // page
    n_blocks = n_pages // ppb
    n_in = blocks_per_step * ppb

    def whole(a):
        return pl.BlockSpec(a.shape, lambda i, j, pt: (0,) * a.ndim)

    key_pages = [pl.BlockSpec((1, page, MOBA_HEADS, HEAD_DIM),
                              lambda i, j, pt, k=k: (pt[i, j * n_in + k], 0, 0, 0)) for k in range(n_in)]
    grid_spec = pltpu.PrefetchScalarGridSpec(
        num_scalar_prefetch=1,
        grid=(bs, n_blocks // blocks_per_step),
        in_specs=[pl.BlockSpec(memory_space=pl.ANY),
                  pl.BlockSpec((1, HEAD_ROWS, HEAD_DIM), lambda i, j, pt: (i, 0, 0)),
                  whole(pe2), whole(w1), whole(w2), whole(pool_nat)] + key_pages,
        out_specs=(pl.BlockSpec((1, HEAD_ROWS, HEAD_DIM), lambda i, j, pt: (i, 0, 0)),
                   pl.BlockSpec((1, HEAD_ROWS, nb), lambda i, j, pt: (i, 0, 0)),
                   pl.BlockSpec((1, blocks_per_step, MOBA_HEADS, HEAD_DIM), lambda i, j, pt: (i, j, 0, 0))),
        scratch_shapes=[pltpu.VMEM((NSA_KV_HEADS, past, HEAD_DIM), F32),
                        pltpu.VMEM((NSA_KV_HEADS, past, HEAD_DIM), F32),
                        pltpu.VMEM((NSA_KV_HEADS * n_sub, half), BF16),
                        pltpu.VMEM((NSA_KV_HEADS * n_sub, half), BF16),
                        pltpu.VMEM((NSA_KV_HEADS, n_sub, HEAD_DIM), BF16),
                        pltpu.VMEM((NSA_KV_HEADS, n_sub, HEAD_DIM), BF16),
                        pltpu.SemaphoreType.DMA((2,))],
    )
    return pl.pallas_call(
        functools.partial(_decode_cmp_kernel, n_pages=n_pages, page=page, n_cmp=n_cmp, qpos=qpos, n_in=n_in),
        out_shape=(jax.ShapeDtypeStruct((bs, HEAD_ROWS, HEAD_DIM), F32),
                   jax.ShapeDtypeStruct((bs, HEAD_ROWS, nb), F32),
                   jax.ShapeDtypeStruct((bs, n_blocks, MOBA_HEADS, HEAD_DIM), F32)),
        grid_spec=grid_spec,
        compiler_params=_cparams(("arbitrary", "arbitrary")),
        name="decode_cmp",
    )(page_table, cache_view, q16, pe2, w1, w2, pool_nat, *([moba_view] * n_in))


def _top_indices(score, blk, n_iter):
    rows = []
    s = score
    blk = blk.astype(F32)
    n_blk = float(score.shape[0])
    for _ in range(n_iter):
        mx = jnp.max(s, axis=0, keepdims=True)
        idx = jnp.min(jnp.where(s == mx, blk, n_blk), axis=0, keepdims=True)
        rows.append(jnp.where(mx > -jnp.inf, idx, -1.0))
        s = jnp.where(blk == idx, -jnp.inf, s)
    return rows


def _decode_select_kernel(imp_ref, km_ref, q_ref, nidx_ref, midx_ref, *, qpos):
    nbk = imp_ref.shape[0]
    blk = lax.broadcasted_iota(jnp.int32, (nbk, 1), 0)
    own = qpos // SEL_BLOCK
    open_blk = blk * SEL_BLOCK <= qpos
    forced = open_blk & ((blk == 0) | (blk == own) | (blk == own - 1))
    score = jnp.where(forced, jnp.inf, jnp.where(open_blk, imp_ref[...], -jnp.inf))
    nidx_ref[...] = jnp.concatenate(_top_indices(score, blk, N_SEL), axis=0).astype(jnp.int32)

    n_seq, n_blocks = km_ref.shape[0], km_ref.shape[1]
    n_col = n_seq * MOBA_HEADS
    lane = lax.broadcasted_iota(jnp.int32, (1, n_col), 1)

    def body(i, sct):
        q = _bf16_round(q_ref[i])
        for h in range(MOBA_HEADS):
            prod = _bf16_round(km_ref[i, :, h, :]) * q[h:h + 1]
            col = jnp.sum(prod, axis=1, keepdims=True)
            sct = jnp.where(lane == i * MOBA_HEADS + h, col, sct)
        return sct

    sct = lax.fori_loop(0, n_seq, body, jnp.zeros((n_blocks, n_col), F32))
    blk_m = lax.broadcasted_iota(jnp.int32, (n_blocks, 1), 0)
    score_m = jnp.where(blk_m < qpos // MOBA_BLOCK, sct, -jnp.inf)
    rows = _top_indices(score_m, blk_m, MOBA_TOPK)
    rows = rows + [jnp.full((1, n_col), -1.0, F32)] * (midx_ref.shape[0] - MOBA_TOPK)
    midx_ref[...] = jnp.concatenate(rows, axis=0).astype(jnp.int32)


def decode_select(imp_t, kmean, q16, *, qpos):
    n_col_a = imp_t.shape[1]
    n_col_b = kmean.shape[0] * MOBA_HEADS

    def whole(a):
        return pl.BlockSpec(a.shape, lambda i: (0,) * a.ndim)

    return pl.pallas_call(
        functools.partial(_decode_select_kernel, qpos=qpos),
        out_shape=(jax.ShapeDtypeStruct((N_SEL, n_col_a), jnp.int32),
                   jax.ShapeDtypeStruct((8, n_col_b), jnp.int32)),
        grid=(1,),
        in_specs=[whole(imp_t), whole(kmean), whole(q16)],
        out_specs=(pl.BlockSpec((N_SEL, n_col_a), lambda i: (0, 0)),
                   pl.BlockSpec((8, n_col_b), lambda i: (0, 0))),
        compiler_params=_cparams(("arbitrary",)),
        name="decode_select",
    )(imp_t, kmean, q16)


def _decode_attend_kernel(pt_ref, nidx_ref, midx_ref, nsa_ref, moba_ref,
                          qa_ref, qb_ref, ga_ref, za_ref, zb_ref, ocmp_ref, newa_ref, newb_ref, win_ref,
                          oa_ref, ob_ref, ks, vs, km, vm, sem, *, page, past, win_base):
    s = pl.program_id(0)
    n_seq = pl.num_programs(0)
    slot = lax.rem(s, 2)
    scale = HEAD_DIM ** -0.5
    blocks_per_page = page // SEL_BLOCK
    pages_per_block = MOBA_BLOCK // page
    n_cache_blk = past // SEL_BLOCK

    def nsa_copies(seq, sl, fn):
        for g in range(NSA_KV_HEADS):
            def body(i, c, g=g):
                blk = jnp.clip(nidx_ref[i, seq * NSA_KV_HEADS + g], 0, n_cache_blk - 1)
                pid = pt_ref[seq, blk // blocks_per_page]
                rows = pl.ds((blk % blocks_per_page) * SEL_BLOCK, SEL_BLOCK)
                dst = pl.ds(i * SEL_BLOCK, SEL_BLOCK)
                fn(pltpu.make_async_copy(nsa_ref.at[pid, rows, 2 * NSA_KV_HEADS + g, :],
                                         ks.at[sl, g, dst, :], sem.at[sl, 0]), 0)
                fn(pltpu.make_async_copy(nsa_ref.at[pid, rows, 3 * NSA_KV_HEADS + g, :],
                                         vs.at[sl, g, dst, :], sem.at[sl, 0]), 1)
                return c
            lax.fori_loop(0, N_SEL, body, 0)

    def moba_copies(seq, sl, fn):
        for h in range(MOBA_HEADS):
            def body(i, c, h=h):
                blk = jnp.maximum(midx_ref[i, seq * MOBA_HEADS + h], 0)
                for pp in range(pages_per_block):
                    pid = pt_ref[seq, blk * pages_per_block + pp]
                    dst = pl.ds((i * pages_per_block + pp) * page, page)
                    fn(pltpu.make_async_copy(moba_ref.at[pid, :, h, :], km.at[sl, h, dst, :], sem.at[sl, 1]), 0)
                    fn(pltpu.make_async_copy(moba_ref.at[pid, :, MOBA_HEADS + h, :],
                                             vm.at[sl, h, dst, :], sem.at[sl, 1]), 1)
                return c
            lax.fori_loop(0, MOBA_TOPK, body, 0)

    def start(cp, lane):
        cp.start(priority=lane % 2)

    def wait(cp, lane):
        cp.wait()

    @pl.when(s == 0)
    def _():
        nsa_copies(0, 0, start)
        moba_copies(0, 0, start)

    @pl.when(s + 1 < n_seq)
    def _():
        nsa_copies(s + 1, 1 - slot, start)
        moba_copies(s + 1, 1 - slot, start)

    nsa_copies(s, slot, wait)
    moba_copies(s, slot, wait)

    row = lax.broadcasted_iota(jnp.int32, (HEAD_ROWS, 1), 0)
    row_g = row >> 2
    qa = qa_ref[0].astype(BF16)
    qa_f = qa.astype(F32)
    new_a = _bf16_round(newa_ref[0])

    n_keys = N_SEL * SEL_BLOCK
    lane_blk = lax.broadcasted_iota(jnp.int32, (1, n_keys), 1) >> (SEL_BLOCK.bit_length() - 1)
    o_slc = []
    for g in range(NSA_KV_HEADS):
        k = ks[slot, g].astype(BF16)
        v = vs[slot, g].astype(BF16)
        sc = lax.dot_general(qa, k, NT_DIMS, preferred_element_type=F32) * scale
        okv = jnp.zeros((1, n_keys), jnp.int32)
        for i in range(N_SEL):
            b = nidx_ref[i, s * NSA_KV_HEADS + g]
            ok = ((b >= 0) & (b < n_cache_blk)).astype(jnp.int32)
            okv = jnp.where(lane_blk == i, ok, okv)
        s_new = jnp.sum(qa_f * new_a[g:g + 1], axis=1, keepdims=True) * scale
        pn, pn_new = _softmax_rows_with_extra(sc, okv > 0, s_new)
        o_slc.append(jnp.dot(pn.astype(BF16), v, preferred_element_type=F32)
                     + _bf16_round(pn_new) * new_a[NSA_KV_HEADS + g:NSA_KV_HEADS + g + 1])
    o_slc = jnp.where(row_g == 0, o_slc[0], o_slc[1])

    half_w = win_ref.shape[1]
    pos_e = win_base + 2 * lax.broadcasted_iota(jnp.int32, (1, half_w), 1)
    o_win = []
    for g in range(NSA_KV_HEADS):
        parts = []
        for parity in range(2):
            k = win_ref[0, :, 4 * parity + g, :].astype(BF16)
            sc = lax.dot_general(qa, k, NT_DIMS, preferred_element_type=F32) * scale
            pos = pos_e + parity
            ok = (pos <= past) & (pos > past - WINDOW) & (pos >= win_base)
            parts.append(jnp.where(ok, sc, -jnp.inf))
        s_new = jnp.sum(qa_f * new_a[4 + g:5 + g], axis=1, keepdims=True) * scale
        sc = jnp.concatenate(parts, axis=1)
        pn, pn_new = _softmax_rows_with_extra(sc, sc > -jnp.inf, s_new)
        o = _bf16_round(pn_new) * new_a[6 + g:7 + g]
        for parity in range(2):
            v = win_ref[0, :, 4 * parity + 2 + g, :].astype(BF16)
            o = o + jnp.dot(pn[:, parity * half_w:(parity + 1) * half_w].astype(BF16), v,
                            preferred_element_type=F32)
        o_win.append(o)
    o_win = jnp.where(row_g == 0, o_win[0], o_win[1])

    gates = ga_ref[0]
    o_a = gates[:, 0:1] * ocmp_ref[0] + gates[:, 1:2] * o_slc + gates[:, 2:3] * o_win
    oa_ref[0] = (o_a * za_ref[0]).astype(BF16)

    qb = qb_ref[0].astype(BF16)
    qb_f = qb.astype(F32)
    new_b = _bf16_round(newb_ref[0])
    n_keys_b = MOBA_TOPK * MOBA_BLOCK
    lane_blk_b = lax.broadcasted_iota(jnp.int32, (1, n_keys_b), 1) >> (MOBA_BLOCK.bit_length() - 1)
    o_b = jnp.zeros((HEAD_ROWS, HEAD_DIM), F32)
    for h in range(MOBA_HEADS):
        k = km[slot, h].astype(BF16)
        v = vm[slot, h].astype(BF16)
        sc = lax.dot_general(qb, k, NT_DIMS, preferred_element_type=F32) * scale
        okv = jnp.zeros((1, n_keys_b), jnp.int32)
        for i in range(MOBA_TOPK):
            ok = (midx_ref[i, s * MOBA_HEADS + h] >= 0).astype(jnp.int32)
            okv = jnp.where(lane_blk_b == i, ok, okv)
        s_new = jnp.sum(qb_f * new_b[h:h + 1], axis=1, keepdims=True) * scale
        pn, pn_new = _softmax_rows_with_extra(sc, okv > 0, s_new)
        o = (jnp.dot(pn.astype(BF16), v, preferred_element_type=F32)
             + _bf16_round(pn_new) * new_b[MOBA_HEADS + h:MOBA_HEADS + h + 1])
        o_b = jnp.where(row == h, o, o_b)
    ob_ref[0] = (o_b * zb_ref[0]).astype(BF16)


def decode_attend(nsa_view, moba_view, page_table, nidx, midx, per_seq, win_view, *, past, win_base):
    bs = page_table.shape[0]
    page = nsa_view.shape[1]
    row_spec = pl.BlockSpec((1, HEAD_ROWS, HEAD_DIM), lambda i, *_: (i, 0, 0))
    grid_spec = pltpu.PrefetchScalarGridSpec(
        num_scalar_prefetch=3,
        grid=(bs,),
        in_specs=[pl.BlockSpec(memory_space=pl.ANY), pl.BlockSpec(memory_space=pl.ANY)]
        + [row_spec] * len(per_seq)
        + [pl.BlockSpec((1,) + win_view.shape[1:], lambda i, *_: (i, 0, 0, 0))],
        out_specs=(row_spec, row_spec),
        scratch_shapes=[pltpu.VMEM((2, NSA_KV_HEADS, N_SEL * SEL_BLOCK, HEAD_DIM), F32),
                        pltpu.VMEM((2, NSA_KV_HEADS, N_SEL * SEL_BLOCK, HEAD_DIM), F32),
                        pltpu.VMEM((2, MOBA_HEADS, MOBA_TOPK * MOBA_BLOCK, HEAD_DIM), F32),
                        pltpu.VMEM((2, MOBA_HEADS, MOBA_TOPK * MOBA_BLOCK, HEAD_DIM), F32),
                        pltpu.SemaphoreType.DMA((2, 2))],
    )
    return pl.pallas_call(
        functools.partial(_decode_attend_kernel, page=page, past=past, win_base=win_base),
        out_shape=(jax.ShapeDtypeStruct((bs, HEAD_ROWS, HEAD_DIM), BF16),
                   jax.ShapeDtypeStruct((bs, HEAD_ROWS, HEAD_DIM), BF16)),
        grid_spec=grid_spec,
        compiler_params=_cparams(("arbitrary",)),
        name="decode_attend",
    )(page_table, nidx, midx, nsa_view, moba_view, *per_seq, win_view)


def _rope_tables(pos):
    half = HEAD_DIM // 2
    inv = ROPE_THETA ** (-(jnp.arange(half, dtype=F32) / half))
    ang = pos.astype(F32)[:, None] * inv[None, :]
    cos, sin = jnp.cos(ang), jnp.sin(ang)
    return jnp.concatenate([cos, cos], axis=1), jnp.concatenate([-sin, sin], axis=1)


def _pool_matrix(nc_pad, n_cmp, n_blk=LANES):
    ratio = SEL_BLOCK // CMP_STRIDE
    n_sub = CMP_BLOCK // CMP_STRIDE
    j = np.arange(n_blk)[:, None]
    n = np.arange(nc_pad)[None, :]
    pool = ((n >= ratio * j - (n_sub - 1)) & (n <= ratio * j + ratio - 1) & (n < n_cmp)).astype(np.float32)
    return jnp.asarray(np.concatenate([pool, pool, pool], axis=1), dtype=BF16)


def _pad_rows(a, rows):
    return jnp.pad(a, ((0, 0), (0, rows - a.shape[1]), (0, 0)))


def kernel(x_prompt, x_sample, cache_nsa, cache_moba, state_nsa_win, page_table, c_prompt, c_sample,
           w_ada, b_ada, norm_g, w_in, pe_k, w_phi_k1, w_phi_k2, pe_v, w_phi_v1, w_phi_v2,
           w_out_a, w_out_b, w_out, final_g):
    depth = w_in.shape[0]
    assert depth == 1, "single-layer step"
    d_model = x_prompt.shape[-1]
    bp, t_p, _ = x_prompt.shape
    bs, t_s, _ = x_sample.shape
    assert bp == 1 and t_s == 1
    page = cache_nsa.shape[2]
    n_pages = page_table.shape[1]
    past = n_pages * page
    wb = state_nsa_win.shape[2]
    w_a = NSA_HEADS * HEAD_DIM
    w_b = MOBA_HEADS * HEAD_DIM
    kv_a = NSA_KV_HEADS * HEAD_DIM
    s0 = w_a + 6 * kv_a
    s1 = s0 + 3 * NSA_HEADS
    s2 = s1 + w_a
    s3 = s2 + 3 * w_b
    s4 = s3 + w_b

    wi = w_in[0]
    w_ga = wi[:, s0:s1].reshape(d_model, NSA_KV_HEADS, 3 * NSA_GROUP)
    w_ga = jnp.pad(w_ga, ((0, 0), (0, 0), (0, LANES - 3 * NSA_GROUP))).reshape(d_model, -1)
    w_all = jnp.concatenate([wi[:, :s0], wi[:, s2:s3], wi[:, s1:s2], wi[:, s3:s4], wi[:, s4:], w_ga],
                            axis=1).astype(BF16)
    c_nsa = slice(0, s0)
    c_moba = slice(c_nsa.stop, c_nsa.stop + 3 * w_b)
    c_z = slice(c_moba.stop, c_moba.stop + w_a + w_b)
    c_gm = slice(c_z.stop, c_z.stop + 2 * d_model)
    c_ga = slice(c_gm.stop, c_gm.stop + NSA_KV_HEADS * LANES)
    w1 = jnp.stack([w_phi_k1[0], w_phi_v1[0]]).astype(BF16)
    w2 = jnp.stack([w_phi_k2[0], w_phi_v2[0]]).astype(BF16)
    half_w = CMP_STRIDE * HEAD_DIM
    pe2 = jnp.stack([pe_k[0].reshape(2, half_w), pe_v[0].reshape(2, half_w)])
    woa = w_out_a[0].astype(BF16)
    wob = w_out_b[0].astype(BF16)
    wo = w_out[0].astype(BF16)
    fg = final_g.reshape(1, d_model)
    ng = norm_g[0].reshape(1, d_model)

    c_all = jnp.concatenate([c_prompt, c_sample], axis=0)
    n_c = c_all.shape[0]
    c_all = jnp.pad(c_all, ((0, -n_c % 8), (0, 0)))
    mod = ada_mod(c_all, w_ada[0], b_ada[0].reshape(1, -1))
    shift, scale, gate = mod[:, :d_model], mod[:, d_model:2 * d_model], mod[:, 2 * d_model:]

    def project(x2d, rows, pos, tm):
        sl = slice(rows.start, rows.stop)
        h = norm_modulate(x2d, ng, scale[sl], shift[sl], tm)
        cos2, sin2 = _rope_tables(pos)
        nsa_f, nsa_b = proj_rope(h, w_all, c_nsa, cos2, sin2, tm, 512, w_a // 512, 2)
        moba_f, moba_b = proj_rope(h, w_all, c_moba, cos2, sin2, tm, 512, 2 * w_b // 512, 0)
        z = proj_act(h, w_all, c_z, "silu", tm, 512)
        g_m = proj_act(h, w_all, c_gm, "sigmoid", tm, 512)
        g_a = proj_act(h, w_all, c_ga, "sigmoid", tm, NSA_KV_HEADS * LANES)
        return nsa_f, nsa_b, moba_f, moba_b, z, g_m, g_a

    def finish(oz_a, oz_b, g_m, x2d, rows, tm):
        merged = merge_branches(oz_a, oz_b, g_m, woa, wob, tm)
        return final_out(merged, x2d, gate[rows.start:rows.stop], wo, fg, tm)

    assert t_p % MOBA_BLOCK == 0
    xp = x_prompt.reshape(t_p, d_model)
    tm_p = 1024 if t_p % 1024 == 0 else 256
    nsa_f, nsa_b, moba_f, moba_b, z_p, gm_p, ga_p = project(
        xp, slice(0, 1), jnp.arange(t_p, dtype=jnp.int32), tm_p)
    n_cmp_p = (t_p - CMP_BLOCK) // CMP_STRIDE + 1
    n_sub_p = t_p // CMP_STRIDE
    assert n_cmp_p == n_sub_p - 1 and n_sub_p % 8 == 0
    sub_p = nsa_f[:, w_a:w_a + 2 * kv_a].reshape(n_sub_p, CMP_STRIDE, 4, HEAD_DIM)
    sub_p = sub_p.transpose(2, 0, 1, 3).reshape(1, 4, n_sub_p, half_w)
    kc_p = compress(sub_p, pe2, w1, w2)
    pool_p = _pool_matrix(n_sub_p, n_cmp_p)
    tk_p = max(t for t in (1024, 512, 256) if t_p % t == 0)
    cq_p = min(256, tk_p)
    assert (t_p - 1) // SEL_BLOCK < LANES
    win_p = jnp.pad(nsa_b[:, w_a + 4 * kv_a:], ((WINDOW, 0), (0, 0)))[None]
    nsa_b3 = nsa_b[None]
    oz_a = nsa_attention(
        nsa_b3, 0, ga_p[None], z_p[None], 0, kc_p,
        nsa_b3, (w_a + 2 * kv_a) // LANES, (w_a + 3 * kv_a) // LANES,
        win_p, 0, NSA_KV_HEADS, pool_p,
        cq=cq_p, tk=tk_p, qpos0=0, n_cmp=n_cmp_p, win_base=0)
    moba_f3 = moba_f[None]
    moba_b3 = moba_b[None]
    km_p = block_means(moba_f3, 1, w_b, blocks_per_step=8 if (t_p // MOBA_BLOCK) % 8 == 0 else t_p // MOBA_BLOCK)
    km_p = _pad_rows(km_p, LANES).astype(BF16)
    cqm_p = tk_p
    oz_b = moba_attention(moba_b3, 0, z_p[None], w_a // LANES, km_p, moba_b3, w_b // LANES,
                          2 * w_b // LANES, cq=cqm_p, tk=tk_p, qpos0=0)
    y_prompt = finish(oz_a[0], oz_b[0], gm_p, xp, slice(0, 1), 256).reshape(1, t_p, d_model)
    new_nsa_prompt = nsa_f[:, w_a:w_a + 4 * kv_a].reshape(1, 1, t_p, 4, NSA_KV_HEADS, HEAD_DIM)
    new_moba_prompt = moba_f[:, w_b:].reshape(1, 1, t_p, 2, MOBA_HEADS, HEAD_DIM)
    wlen = min(WINDOW, t_p)
    new_win_prompt = nsa_f[t_p - wlen:, w_a + 4 * kv_a:].reshape(1, 1, wlen, 2, NSA_KV_HEADS, HEAD_DIM)

    xs = x_sample.reshape(bs, d_model)
    pos_s = jnp.full((bs,), past, jnp.int32)
    nsa_fs, _, moba_fs, _, z_s, gm_s, ga_s = project(xs, slice(1, 1 + bs), pos_s, bs)
    l_s = past + 1
    n_cmp_s = (l_s - CMP_BLOCK) // CMP_STRIDE + 1
    n_sub_s = past // CMP_STRIDE
    assert n_cmp_s == n_sub_s - 1 and n_sub_s % 8 == 0
    assert past % MOBA_BLOCK == 0 and MOBA_BLOCK % page == 0 and page % SEL_BLOCK == 0
    assert wb == WINDOW and wb % 2 == 0 and past >= WINDOW
    n_phys = cache_nsa.shape[1]
    nsa_view = cache_nsa.reshape(n_phys, page, 4 * NSA_KV_HEADS, HEAD_DIM)
    moba_view = cache_moba.reshape(n_phys, page, 2 * MOBA_HEADS, HEAD_DIM)
    win_view = state_nsa_win.reshape(bs, wb // 2, 4 * NSA_KV_HEADS, HEAD_DIM)

    def head_rows(a2d):
        a = a2d.reshape(bs, -1, HEAD_DIM)
        return jnp.pad(a, ((0, 0), (0, HEAD_ROWS - a.shape[1]), (0, 0)))

    qa16 = head_rows(nsa_fs[:, :w_a])
    qb16 = head_rows(moba_fs[:, :w_b])
    za16 = head_rows(z_s[:, :w_a])
    zb16 = head_rows(z_s[:, w_a:])
    ga16 = ga_s.reshape(bs, NSA_KV_HEADS, LANES)[:, :, :3 * NSA_GROUP].reshape(bs, NSA_HEADS, 3)
    ga16 = jnp.pad(ga16, ((0, 0), (0, HEAD_ROWS - NSA_HEADS), (0, LANES - 3)))
    new_a16 = head_rows(nsa_fs[:, w_a + 2 * kv_a:])
    new_b16 = head_rows(moba_fs[:, w_b:])

    n_sel_blk = -(-(past // SEL_BLOCK + 1) // LANES) * LANES
    pool_s = _pool_matrix(n_sub_s, n_cmp_s, n_sel_blk).T
    ocmp16, imp, kmean_s = decode_cmp(nsa_view, moba_view, page_table, qa16, pe2, w1, w2, pool_s,
                                      n_cmp=n_cmp_s, qpos=past,
                                      blocks_per_step=4 if (past // MOBA_BLOCK) % 4 == 0 else 1)
    imp_t = imp[:, :NSA_KV_HEADS, :].transpose(2, 0, 1).reshape(n_sel_blk, bs * NSA_KV_HEADS)
    nidx, midx = decode_select(imp_t, kmean_s, qb16, qpos=past)
    oa16, ob16 = decode_attend(nsa_view, moba_view, page_table, nidx, midx,
                               [qa16, qb16, ga16, za16, zb16, ocmp16, new_a16, new_b16], win_view,
                               past=past, win_base=past - wb)
    oz_as = oa16[:, :NSA_HEADS].reshape(bs, w_a)
    oz_bs = ob16[:, :MOBA_HEADS].reshape(bs, w_b)
    y_sample = finish(oz_as, oz_bs, gm_s, xs, slice(1, 1 + bs), bs).reshape(bs, 1, d_model)
    new_nsa_sample = nsa_fs[:, w_a:w_a + 4 * kv_a].reshape(1, bs, 1, 4, NSA_KV_HEADS, HEAD_DIM)
    new_moba_sample = moba_fs[:, w_b:].reshape(1, bs, 1, 2, MOBA_HEADS, HEAD_DIM)
    new_win_sample = jnp.concatenate(
        [state_nsa_win[:, :, 1:],
         nsa_fs[:, w_a + 4 * kv_a:].reshape(1, bs, 1, 2, NSA_KV_HEADS, HEAD_DIM)], axis=2)

    return (y_prompt, y_sample, new_nsa_prompt, new_nsa_sample, new_moba_prompt, new_moba_sample,
            new_win_prompt, new_win_sample)
```

```python
import functools

import numpy as np
import jax
import jax.numpy as jnp
from jax import lax
from jax.experimental import pallas as pl
from jax.experimental.pallas import tpu as pltpu

F32 = jnp.float32
BF16 = jnp.bfloat16

HEAD_DIM = 128
NSA_HEADS = 8
NSA_KV_HEADS = 2
NSA_GROUP = NSA_HEADS // NSA_KV_HEADS
CMP_BLOCK = 32
CMP_STRIDE = 16
SEL_BLOCK = 64
N_SEL = 16
WINDOW = 512
PHI_HIDDEN = 2 * HEAD_DIM
MOBA_HEADS = 8
MOBA_BLOCK = 256
MOBA_TOPK = 3
ROPE_THETA = 10000.0
NORM_EPS = 1e-6

LANES = 128
MASK_BIAS = -(2.0 ** 100)
M_INIT = -1e30
LOG2_E = 1.4426950408889634
VMEM_LIMIT = 56 * 1024 * 1024

NT_DIMS = (((1,), (1,)), ((), ()))
TN_DIMS = (((0,), (0,)), ((), ()))


def _cparams(sem):
    return pltpu.CompilerParams(dimension_semantics=sem, vmem_limit_bytes=VMEM_LIMIT)


def _sigmoid(x):
    return 1.0 / (1.0 + jnp.exp(-x))


def _ada_kernel(c_ref, w_ref, b_ref, o_ref):
    c = c_ref[...]
    a = (c * _sigmoid(c)).astype(BF16)
    o_ref[...] = jnp.dot(a, w_ref[...].astype(BF16), preferred_element_type=F32) + b_ref[...]


def ada_mod(c, w, b, tn=768):
    m, k = c.shape
    n = w.shape[1]
    return pl.pallas_call(
        _ada_kernel,
        out_shape=jax.ShapeDtypeStruct((m, n), F32),
        grid=(n // tn,),
        in_specs=[pl.BlockSpec((m, k), lambda j: (0, 0)),
                  pl.BlockSpec((k, tn), lambda j: (0, j)),
                  pl.BlockSpec((1, tn), lambda j: (0, j))],
        out_specs=pl.BlockSpec((m, tn), lambda j: (0, j)),
        compiler_params=_cparams(("arbitrary",)),
        name="ada_mod",
    )(c, w, b)


def _h_kernel(x_ref, g_ref, sc_ref, sh_ref, o_ref):
    x = x_ref[...]
    ms = jnp.mean(x * x, axis=-1, keepdims=True)
    h = x * lax.rsqrt(ms + NORM_EPS) * g_ref[...]
    o_ref[...] = (h * (1.0 + sc_ref[...]) + sh_ref[...]).astype(BF16)


def norm_modulate(x, g, scale, shift, tm):
    m, d = x.shape
    per_row = scale.shape[0] != 1
    mod_spec = (pl.BlockSpec((tm, d), lambda i: (i, 0)) if per_row
                else pl.BlockSpec((1, d), lambda i: (0, 0)))
    return pl.pallas_call(
        _h_kernel,
        out_shape=jax.ShapeDtypeStruct((m, d), BF16),
        grid=(m // tm,),
        in_specs=[pl.BlockSpec((tm, d), lambda i: (i, 0)),
                  pl.BlockSpec((1, d), lambda i: (0, 0)),
                  mod_spec, mod_spec],
        out_specs=pl.BlockSpec((tm, d), lambda i: (i, 0)),
        compiler_params=_cparams(("arbitrary",)),
        name="norm_modulate",
    )(x, g, scale, shift)


def _proj_rope_kernel(h_ref, w_ref, cos_ref, sin_ref, of_ref, ob_ref, *, full_tiles, partial):
    acc = jnp.dot(h_ref[...], w_ref[...], preferred_element_type=F32)
    all_heads = pl.program_id(1) < full_tiles
    cos = cos_ref[...]
    sin = sin_ref[...]
    for t in range(acc.shape[1] // LANES):
        a = acc[:, t * LANES:(t + 1) * LANES]
        r = a * cos + pltpu.roll(a, HEAD_DIM // 2, axis=1) * sin
        out = r if t < partial else jnp.where(all_heads, r, a)
        of_ref[:, t * LANES:(t + 1) * LANES] = out
        ob_ref[:, t * LANES:(t + 1) * LANES] = out.astype(BF16)


def proj_rope(h, w, cols, cos2, sin2, tm, tn, full_tiles, partial):
    m, k = h.shape
    n = cols.stop - cols.start
    off = cols.start // tn
    assert cols.start % tn == 0 and n % tn == 0
    return pl.pallas_call(
        functools.partial(_proj_rope_kernel, full_tiles=full_tiles, partial=partial),
        out_shape=(jax.ShapeDtypeStruct((m, n), F32), jax.ShapeDtypeStruct((m, n), BF16)),
        grid=(m // tm, n // tn),
        in_specs=[pl.BlockSpec((tm, k), lambda i, j: (i, 0)),
                  pl.BlockSpec((k, tn), lambda i, j: (0, off + j)),
                  pl.BlockSpec((tm, LANES), lambda i, j: (i, 0)),
                  pl.BlockSpec((tm, LANES), lambda i, j: (i, 0))],
        out_specs=(pl.BlockSpec((tm, tn), lambda i, j: (i, j)),
                   pl.BlockSpec((tm, tn), lambda i, j: (i, j))),
        compiler_params=_cparams(("arbitrary", "arbitrary")),
        name="proj_rope",
    )(h, w, cos2, sin2)


def _proj_act_kernel(h_ref, w_ref, o_ref, *, act):
    acc = jnp.dot(h_ref[...], w_ref[...], preferred_element_type=F32)
    s = _sigmoid(acc)
    o_ref[...] = (acc * s if act == "silu" else s).astype(o_ref.dtype)


def proj_act(h, w, cols, act, tm, tn):
    m, k = h.shape
    n = cols.stop - cols.start
    off = cols.start // tn
    assert cols.start % tn == 0 and n % tn == 0
    return pl.pallas_call(
        functools.partial(_proj_act_kernel, act=act),
        out_shape=jax.ShapeDtypeStruct((m, n), F32),
        grid=(m // tm, n // tn),
        in_specs=[pl.BlockSpec((tm, k), lambda i, j: (i, 0)),
                  pl.BlockSpec((k, tn), lambda i, j: (0, off + j))],
        out_specs=pl.BlockSpec((tm, tn), lambda i, j: (i, j)),
        compiler_params=_cparams(("arbitrary", "arbitrary")),
        name="proj_" + act,
    )(h, w)


def _compress_kernel(s_ref, pe_ref, w1_ref, w2_ref, o_ref):
    s = s_ref[0, 0]
    half = s.shape[1]
    top = jnp.dot((s + pe_ref[0, 0:1]).astype(BF16), w1_ref[0, :half], preferred_element_type=F32)
    bot = jnp.dot((s + pe_ref[0, 1:2]).astype(BF16), w1_ref[0, half:], preferred_element_type=F32)
    hid = top + pltpu.roll(bot, s.shape[0] - 1, axis=0)
    hid = (hid * _sigmoid(hid)).astype(BF16)
    o_ref[0, 0] = jnp.dot(hid, w2_ref[0], preferred_element_type=F32).astype(BF16)


def compress(sub, pe2, w1, w2):
    b, four, n_sub, width = sub.shape
    return pl.pallas_call(
        _compress_kernel,
        out_shape=jax.ShapeDtypeStruct((b, four, n_sub, HEAD_DIM), BF16),
        grid=(four, b),
        in_specs=[pl.BlockSpec((1, 1, n_sub, width), lambda j, i: (i, j, 0, 0)),
                  pl.BlockSpec((1, 2, width), lambda j, i: (j // 2, 0, 0)),
                  pl.BlockSpec((1, 2 * width, PHI_HIDDEN), lambda j, i: (j // 2, 0, 0)),
                  pl.BlockSpec((1, PHI_HIDDEN, HEAD_DIM), lambda j, i: (j // 2, 0, 0))],
        out_specs=pl.BlockSpec((1, 1, n_sub, HEAD_DIM), lambda j, i: (i, j, 0, 0)),
        compiler_params=_cparams(("arbitrary", "arbitrary")),
        name="compress",
    )(sub, pe2, w1, w2)


def _kmean_kernel(k_ref, o_ref):
    k = k_ref[0]
    nb = k.shape[0] // MOBA_BLOCK
    o_ref[0] = jnp.mean(k.reshape(nb, MOBA_BLOCK, k.shape[1]), axis=1)


def block_means(rows, col_block, width, blocks_per_step=8):
    b, t, _ = rows.shape
    nb = t // MOBA_BLOCK
    return pl.pallas_call(
        _kmean_kernel,
        out_shape=jax.ShapeDtypeStruct((b, nb, width), F32),
        grid=(b, nb // blocks_per_step),
        in_specs=[pl.BlockSpec((1, blocks_per_step * MOBA_BLOCK, width),
                               lambda i, j: (i, j, col_block))],
        out_specs=pl.BlockSpec((1, blocks_per_step, width), lambda i, j: (i, j, 0)),
        compiler_params=_cparams(("arbitrary", "arbitrary")),
        name="block_means",
    )(rows)


def _masked_softmax_cols(s, valid, scale):
    s = jnp.where(valid, s, -jnp.inf)
    m = jnp.max(s, axis=0, keepdims=True)
    m = jnp.where(m > -jnp.inf, m, 0.0)
    p = jnp.exp2((s - m) * (scale * LOG2_E))
    return p / jnp.maximum(jnp.sum(p, axis=0, keepdims=True), 1e-30)


def _transpose(x):
    rows, cols = x.shape
    pr, pc = -rows % LANES, -cols % LANES
    if pr:
        x = jnp.concatenate([x, jnp.zeros((pr, cols), x.dtype)], axis=0)
    if pc:
        x = jnp.concatenate([x, jnp.zeros((rows + pr, pc), x.dtype)], axis=1)
    return x.T[:cols, :rows]


def _select_top(score, blk, n_iter, limit):
    sel = jnp.zeros(score.shape, jnp.bool_)
    s = score
    blk = blk.astype(F32)
    n_blk = float(score.shape[0])
    for it in range(n_iter):
        mx = jnp.max(s, axis=0, keepdims=True)
        idx = jnp.min(jnp.where(s == mx, blk, n_blk), axis=0, keepdims=True)
        pick = blk == idx
        ok = mx > -jnp.inf
        if limit is not None:
            ok = ok & (limit > it)
        sel = sel | (pick & ok)
        s = jnp.where(pick, -jnp.inf, s)
    return sel


def _block_sparse_flash(qp, k_ref, v_ref, s_a, s_b, tq_lane, cs, tk, blk_shift, scale):
    r = qp.shape[0]
    row = lax.broadcasted_iota(jnp.int32, (tk, LANES), 0)
    lane = lax.broadcasted_iota(jnp.int32, (tk, LANES), 1)
    rel_blk = lane - (row >> blk_shift)
    blocks_per_tile = tk >> blk_shift

    c = scale * LOG2_E
    n_full = cs // tk

    def scores_into(i, dst):
        i = jnp.minimum(i, n_full)
        start = pl.multiple_of(i * tk, tk)
        k_t = k_ref[0, pl.ds(start, tk), :]
        onehot = jnp.where(rel_blk == i * blocks_per_tile, 1.0, 0.0).astype(BF16)
        kp = jnp.concatenate([k_t, onehot], axis=1)
        dst[...] = lax.dot_general(kp, qp, NT_DIMS, preferred_element_type=F32)

    def consume(i, carry, src, causal):
        m, l, acc = carry
        start = pl.multiple_of(i * tk, tk)
        v_t = v_ref[0, pl.ds(start, tk), :]
        s = src[...]
        if causal:
            kpos = start + lax.broadcasted_iota(jnp.int32, (tk, 1), 0)
            s = jnp.where(kpos <= tq_lane, s, M_INIT)
        m_new = jnp.maximum(m, jnp.max(s, axis=0, keepdims=True))
        alpha = jnp.exp2((m - m_new) * c)
        p = jnp.exp2((s - m_new) * c)
        l = alpha * l + jnp.sum(p, axis=0, keepdims=True)
        pv = lax.dot_general(v_t, p.astype(BF16), TN_DIMS, preferred_element_type=F32)
        return m_new, l, alpha * acc + pv

    def pair(jp, carry):
        scores_into(2 * jp + 1, s_b)
        carry = consume(2 * jp, carry, s_a, False)
        scores_into(2 * jp + 2, s_a)
        return consume(2 * jp + 1, carry, s_b, False)

    def odd_tail(carry):
        scores_into(n_full, s_b)
        carry = consume(n_full - 1, carry, s_a, False)
        return consume(n_full, carry, s_b, True)

    init = (jnp.full((1, r), M_INIT, F32), jnp.zeros((1, r), F32), jnp.zeros((HEAD_DIM, r), F32))
    scores_into(0, s_a)
    carry = lax.fori_loop(0, n_full // 2, pair, init)
    _, l, acc = lax.cond(n_full % 2 == 1, odd_tail, lambda cr: consume(n_full, cr, s_a, True), carry)
    return acc / l


def _nsa_kernel(q_ref, ga_ref, z_ref, kc_ref, vc_ref, ks_ref, vs_ref, kw_ref, vw_ref, pool_ref,
                o_ref, s_a, s_b, *, cq, tk, qpos0, n_cmp, win_base):
    c = pl.program_id(2)
    r = NSA_GROUP * cq
    cs = qpos0 + c * cq
    scale = HEAD_DIM ** -0.5
    q = q_ref[0]
    qs = jnp.concatenate([q[:, h * LANES:(h + 1) * LANES] for h in range(NSA_GROUP)], axis=0)
    tq_lane = cs + (lax.broadcasted_iota(jnp.int32, (1, r), 1) & (cq - 1))

    kc = kc_ref[0, 0]
    nc_pad = kc.shape[0]
    n_id = lax.broadcasted_iota(jnp.int32, (nc_pad, 1), 0)
    sc = lax.dot_general(kc, qs, NT_DIMS, preferred_element_type=F32)
    cmp_end = jnp.where(n_id < n_cmp, n_id * CMP_STRIDE + (CMP_BLOCK - 1), jnp.iinfo(jnp.int32).max)
    pc = _masked_softmax_cols(sc, cmp_end <= tq_lane, scale)
    o_cmp = lax.dot_general(vc_ref[0, 0], pc.astype(BF16), TN_DIMS, preferred_element_type=F32)

    psum = pc[:, 0:cq]
    for h in range(1, NSA_GROUP):
        psum = psum + pc[:, h * cq:(h + 1) * cq]
    hi = psum.astype(BF16)
    r1 = psum - hi.astype(F32)
    mid = r1.astype(BF16)
    lo = (r1 - mid.astype(F32)).astype(BF16)
    imp = jnp.dot(pool_ref[...], jnp.concatenate([hi, mid, lo], axis=0), preferred_element_type=F32)

    blk = lax.broadcasted_iota(jnp.int32, (LANES, 1), 0)
    tq = cs + lax.broadcasted_iota(jnp.int32, (1, cq), 1)
    own = tq >> 6
    open_blk = blk * SEL_BLOCK <= tq
    forced = open_blk & ((blk == 0) | (blk == own) | (blk == own - 1))
    score = jnp.where(forced, jnp.inf, jnp.where(open_blk, imp, -jnp.inf))
    n_beyond = (own >= LANES).astype(jnp.int32) + (own - 1 >= LANES).astype(jnp.int32)
    sel = _select_top(score, blk, N_SEL, N_SEL - n_beyond)
    bias = _transpose(jnp.where(sel, 0.0, MASK_BIAS)).astype(BF16)
    qp = jnp.concatenate([qs, jnp.concatenate([bias] * NSA_GROUP, axis=0)], axis=1)
    o_slc = _block_sparse_flash(qp, ks_ref, vs_ref, s_a, s_b, tq_lane, cs, tk, 6, scale)

    tw = WINDOW + cq
    w_start = pl.multiple_of(c * cq, cq)
    k_w = kw_ref[0, pl.ds(w_start, tw), :]
    v_w = vw_ref[0, pl.ds(w_start, tw), :]
    wpos = (cs - WINDOW) + lax.broadcasted_iota(jnp.int32, (tw, 1), 0)
    sw = lax.dot_general(k_w, qs, NT_DIMS, preferred_element_type=F32)
    wpos = jnp.where(wpos >= win_base, wpos, jnp.iinfo(jnp.int32).min // 2)
    wvalid = (tq_lane - wpos).astype(jnp.uint32) < WINDOW
    pw = _masked_softmax_cols(sw, wvalid, scale)
    o_win = lax.dot_general(v_w, pw.astype(BF16), TN_DIMS, preferred_element_type=F32)

    g = _transpose(ga_ref[0])
    for h in range(NSA_GROUP):
        cols = slice(h * cq, (h + 1) * cq)
        o = (g[3 * h:3 * h + 1] * o_cmp[:, cols] + g[3 * h + 1:3 * h + 2] * o_slc[:, cols]
             + g[3 * h + 2:3 * h + 3] * o_win[:, cols])
        o_ref[0, :, h * LANES:(h + 1) * LANES] = (
            _transpose(o) * z_ref[0, :, h * LANES:(h + 1) * LANES]).astype(BF16)


def nsa_attention(q, q_col, gates, z, z_col, kc, ks, ks_col, vs_col, kw, kw_col, vw_col, pool,
                  *, cq, tk, qpos0, n_cmp, win_base):
    b, tq_len, _ = q.shape
    tkv = ks.shape[1]
    tw_len = kw.shape[1]
    nc_pad = kc.shape[2]
    gw = NSA_GROUP * LANES
    kern = functools.partial(_nsa_kernel, cq=cq, tk=tk, qpos0=qpos0, n_cmp=n_cmp, win_base=win_base)
    return pl.pallas_call(
        kern,
        out_shape=jax.ShapeDtypeStruct((b, tq_len, NSA_HEADS * HEAD_DIM), BF16),
        grid=(b, NSA_KV_HEADS, tq_len // cq),
        in_specs=[
            pl.BlockSpec((1, cq, gw), lambda i, g, c: (i, c, q_col + g)),
            pl.BlockSpec((1, cq, LANES), lambda i, g, c: (i, c, g)),
            pl.BlockSpec((1, cq, gw), lambda i, g, c: (i, c, z_col + g)),
            pl.BlockSpec((1, 1, nc_pad, HEAD_DIM), lambda i, g, c: (i, g, 0, 0)),
            pl.BlockSpec((1, 1, nc_pad, HEAD_DIM), lambda i, g, c: (i, 2 + g, 0, 0)),
            pl.BlockSpec((1, tkv, HEAD_DIM), lambda i, g, c: (i, 0, ks_col + g)),
            pl.BlockSpec((1, tkv, HEAD_DIM), lambda i, g, c: (i, 0, vs_col + g)),
            pl.BlockSpec((1, tw_len, HEAD_DIM), lambda i, g, c: (i, 0, kw_col + g)),
            pl.BlockSpec((1, tw_len, HEAD_DIM), lambda i, g, c: (i, 0, vw_col + g)),
            pl.BlockSpec(pool.shape, lambda i, g, c: (0, 0)),
        ],
        out_specs=pl.BlockSpec((1, cq, gw), lambda i, g, c: (i, c, g)),
        scratch_shapes=[pltpu.VMEM((tk, NSA_GROUP * cq), F32), pltpu.VMEM((tk, NSA_GROUP * cq), F32)],
        compiler_params=_cparams(("arbitrary", "arbitrary", "arbitrary")),
        name="nsa_attention",
    )(q, gates, z, kc, kc, ks, ks, kw, kw, pool)


def _moba_kernel(pt_ref, q_ref, z_ref, km_ref, k_ref, v_ref, pages_ref, o_ref, kms_ref, s_a, s_b, pbuf, sem,
                 *, cq, tk, qpos0, pages_per_step, parts):
    c = pl.program_id(2)
    step = (pl.program_id(0) * pl.num_programs(1) + pl.program_id(1)) * pl.num_programs(2) + c
    seq = step // parts
    first_page = (step % parts) * pages_per_step

    def page_copies(fn):
        def body(p, carry):
            fn(pltpu.make_async_copy(pages_ref.at[pt_ref[seq, first_page + p], :, pl.ds(0, MOBA_HEADS), :],
                                     pbuf.at[p], sem.at[0]))
            return carry
        lax.fori_loop(0, pages_per_step, body, 0)

    page_copies(lambda cp: cp.start())
    cs = qpos0 + c * cq
    scale = HEAD_DIM ** -0.5
    q = q_ref[0]
    tq = cs + lax.broadcasted_iota(jnp.int32, (1, cq), 1)
    blk = lax.broadcasted_iota(jnp.int32, (LANES, 1), 0)
    own = tq >> 8
    sc = lax.dot_general(km_ref[0], q, NT_DIMS, preferred_element_type=F32)
    score = jnp.where(blk < own, sc, -jnp.inf)
    sel = _select_top(score, blk, MOBA_TOPK, None) | (blk == own)
    bias = _transpose(jnp.where(sel, 0.0, MASK_BIAS)).astype(BF16)
    qp = jnp.concatenate([q, bias], axis=1)
    o = _block_sparse_flash(qp, k_ref, v_ref, s_a, s_b, tq, cs, tk, 8, scale)
    o_ref[0] = (_transpose(o) * z_ref[0]).astype(BF16)

    page_copies(lambda cp: cp.wait())
    pages_per_block = MOBA_BLOCK // pbuf.shape[1]
    for b in range(pages_per_step // pages_per_block):
        tot = jnp.sum(pbuf[b * pages_per_block], axis=0)
        for pp in range(1, pages_per_block):
            tot = tot + jnp.sum(pbuf[b * pages_per_block + pp], axis=0)
        kms_ref[0, b] = tot / MOBA_BLOCK


def moba_attention(q, q_col, z, z_col, kmean, k, k_col, v_col, page_table, pages_view, *, cq, tk, qpos0):
    b, tq_len, _ = q.shape
    tkv = k.shape[1]
    bs, n_pages = page_table.shape
    page = pages_view.shape[1]
    n_steps = b * MOBA_HEADS * (tq_len // cq)
    pages_per_step = bs * n_pages // n_steps
    parts = n_pages // pages_per_step
    blocks_per_step = pages_per_step * page // MOBA_BLOCK
    assert pages_per_step * n_steps == bs * n_pages and parts * pages_per_step == n_pages
    assert blocks_per_step * MOBA_BLOCK == pages_per_step * page
    n_c = tq_len // cq

    def step_of(i, h, c):
        return (i * MOBA_HEADS + h) * n_c + c

    kern = functools.partial(_moba_kernel, cq=cq, tk=tk, qpos0=qpos0, pages_per_step=pages_per_step,
                             parts=parts)
    grid_spec = pltpu.PrefetchScalarGridSpec(
        num_scalar_prefetch=1,
        grid=(b, MOBA_HEADS, n_c),
        in_specs=[
            pl.BlockSpec((1, cq, LANES), lambda i, h, c, pt: (i, c, q_col + h)),
            pl.BlockSpec((1, cq, LANES), lambda i, h, c, pt: (i, c, z_col + h)),
            pl.BlockSpec((1, LANES, HEAD_DIM), lambda i, h, c, pt: (i, 0, h)),
            pl.BlockSpec((1, tkv, HEAD_DIM), lambda i, h, c, pt: (i, 0, k_col + h)),
            pl.BlockSpec((1, tkv, HEAD_DIM), lambda i, h, c, pt: (i, 0, v_col + h)),
            pl.BlockSpec(memory_space=pl.ANY),
        ],
        out_specs=(
            pl.BlockSpec((1, cq, LANES), lambda i, h, c, pt: (i, c, h)),
            pl.BlockSpec((1, blocks_per_step, MOBA_HEADS, HEAD_DIM),
                         lambda i, h, c, pt: (step_of(i, h, c) // parts, step_of(i, h, c) % parts, 0, 0)),
        ),
        scratch_shapes=[pltpu.VMEM((tk, cq), F32), pltpu.VMEM((tk, cq), F32),
                        pltpu.VMEM((pages_per_step, page, MOBA_HEADS, HEAD_DIM), F32),
                        pltpu.SemaphoreType.DMA((1,))],
    )
    return pl.pallas_call(
        kern,
        out_shape=(jax.ShapeDtypeStruct((b, tq_len, MOBA_HEADS * HEAD_DIM), BF16),
                   jax.ShapeDtypeStruct((bs, n_pages * page // MOBA_BLOCK, MOBA_HEADS, HEAD_DIM), F32)),
        grid_spec=grid_spec,
        compiler_params=_cparams(("arbitrary", "arbitrary", "arbitrary")),
        name="moba_attention",
    )(page_table, q, z, kmean, k, k, pages_view)


def _merge_kernel(oa_ref, ob_ref, ga_ref, gb_ref, wa_ref, wb_ref, o_ref):
    br_a = jnp.dot(oa_ref[...], wa_ref[...], preferred_element_type=F32)
    br_b = jnp.dot(ob_ref[...], wb_ref[...], preferred_element_type=F32)
    o_ref[...] = (ga_ref[...] * br_a + gb_ref[...] * br_b).astype(BF16)


def merge_branches(oz_a, oz_b, g_m, w_a, w_b, tm):
    m, ka = oz_a.shape
    d = w_a.shape[1]
    return pl.pallas_call(
        _merge_kernel,
        out_shape=jax.ShapeDtypeStruct((m, d), BF16),
        grid=(m // tm,),
        in_specs=[pl.BlockSpec((tm, ka), lambda i: (i, 0)),
                  pl.BlockSpec((tm, ka), lambda i: (i, 0)),
                  pl.BlockSpec((tm, d), lambda i: (i, 0)),
                  pl.BlockSpec((tm, d), lambda i: (i, 1)),
                  pl.BlockSpec((ka, d), lambda i: (0, 0)),
                  pl.BlockSpec((ka, d), lambda i: (0, 0))],
        out_specs=pl.BlockSpec((tm, d), lambda i: (i, 0)),
        compiler_params=_cparams(("arbitrary",)),
        name="merge_branches",
    )(oz_a, oz_b, g_m, g_m, w_a, w_b)


def _final_kernel(m_ref, x_ref, gate_ref, w_ref, g_ref, o_ref):
    y = x_ref[...] + gate_ref[...] * jnp.dot(m_ref[...], w_ref[...], preferred_element_type=F32)
    ms = jnp.mean(y * y, axis=-1, keepdims=True)
    o_ref[...] = y * lax.rsqrt(ms + NORM_EPS) * g_ref[...]


def final_out(merged, x, gate, w, g, tm):
    m, d = x.shape
    per_row = gate.shape[0] != 1
    gate_spec = (pl.BlockSpec((tm, d), lambda i: (i, 0)) if per_row
                 else pl.BlockSpec((1, d), lambda i: (0, 0)))
    return pl.pallas_call(
        _final_kernel,
        out_shape=jax.ShapeDtypeStruct((m, d), F32),
        grid=(m // tm,),
        in_specs=[pl.BlockSpec((tm, d), lambda i: (i, 0)),
                  pl.BlockSpec((tm, d), lambda i: (i, 0)),
                  gate_spec,
                  pl.BlockSpec((d, d), lambda i: (0, 0)),
                  pl.BlockSpec((1, d), lambda i: (0, 0))],
        out_specs=pl.BlockSpec((tm, d), lambda i: (i, 0)),
        compiler_params=_cparams(("arbitrary",)),
        name="final_out",
    )(merged, x, gate, w, g)


HEAD_ROWS = 16


def _softmax_rows_with_extra(s, valid, s_new):
    s = jnp.where(valid, s, -jnp.inf)
    m = jnp.maximum(jnp.max(s, axis=1, keepdims=True), s_new)
    p = jnp.exp(s - m)
    p_new = jnp.exp(s_new - m)
    l = jnp.sum(p, axis=1, keepdims=True) + p_new
    return p / l, p_new / l


def _bf16_round(x):
    return x.astype(BF16).astype(F32)


def _decode_cmp_kernel(pt_ref, cache_ref, q_ref, pe_ref, w1_ref, w2_ref, pool_ref,
                       ocmp_ref, imp_ref, bufk, bufv, xtop, xbot, kc_scr, vc_scr, sem,
                       *, n_pages, page, n_cmp, qpos):
    s = pl.program_id(0)
    j = pl.program_id(1)
    n_seq = pl.num_programs(0)
    last_j = pl.num_programs(1) - 1
    n_sub = n_pages * page // CMP_STRIDE
    half = CMP_STRIDE * HEAD_DIM
    scale = HEAD_DIM ** -0.5

    def page_copies(seq, kv, buf, fn):
        def body(p, c):
            pid = pt_ref[seq, p]
            for g in range(NSA_KV_HEADS):
                fn(pltpu.make_async_copy(cache_ref.at[pid, :, kv * NSA_KV_HEADS + g, :],
                                         buf.at[g, pl.ds(p * page, page), :], sem.at[kv]), g)
            return c
        lax.fori_loop(0, n_pages, body, 0)

    def start(cp, lane):
        cp.start(priority=lane % 2)

    def wait(cp, lane):
        cp.wait()

    def compress_half(buf, kv):
        for r in range(CMP_STRIDE):
            x = jnp.concatenate([buf[g, pl.ds(r, n_sub, stride=CMP_STRIDE), :]
                                 for g in range(NSA_KV_HEADS)], axis=0)
            cols = slice(r * HEAD_DIM, (r + 1) * HEAD_DIM)
            xtop[:, cols] = (x + pe_ref[kv, 0:1, cols]).astype(BF16)
            xbot[:, cols] = (x + pe_ref[kv, 1:2, cols]).astype(BF16)
        top = jnp.dot(xtop[...], w1_ref[kv, :half], preferred_element_type=F32)
        bot = jnp.dot(xbot[...], w1_ref[kv, half:], preferred_element_type=F32)
        outs = []
        for g in range(NSA_KV_HEADS):
            rows = slice(g * n_sub, (g + 1) * n_sub)
            hid = top[rows] + pltpu.roll(bot[rows], n_sub - 1, axis=0)
            hid = (hid * _sigmoid(hid)).astype(BF16)
            outs.append(jnp.dot(hid, w2_ref[kv], preferred_element_type=F32).astype(BF16))
        return outs

    @pl.when(j == 0)
    def _():
        @pl.when(s == 0)
        def _():
            page_copies(0, 0, bufk, start)
            page_copies(0, 1, bufv, start)

        page_copies(s, 0, bufk, wait)
        kc = compress_half(bufk, 0)
        for g in range(NSA_KV_HEADS):
            kc_scr[g] = kc[g]

        @pl.when(s + 1 < n_seq)
        def _():
            page_copies(s + 1, 0, bufk, start)

    @pl.when(j == jnp.minimum(1, last_j))
    def _():
        page_copies(s, 1, bufv, wait)
        vc = compress_half(bufv, 1)
        for g in range(NSA_KV_HEADS):
            vc_scr[g] = vc[g]

        @pl.when(s + 1 < n_seq)
        def _():
            page_copies(s + 1, 1, bufv, start)

    @pl.when(j == jnp.minimum(2, last_j))
    def _():
        q = q_ref[0].astype(BF16)
        row_g = lax.broadcasted_iota(jnp.int32, (HEAD_ROWS, 1), 0) >> 2
        sc = [lax.dot_general(q, kc_scr[g], NT_DIMS, preferred_element_type=F32)
              for g in range(NSA_KV_HEADS)]
        sc = jnp.where(row_g == 0, sc[0], sc[1]) * scale
        n_id = lax.broadcasted_iota(jnp.int32, (1, n_sub), 1)
        valid = ((n_id * CMP_STRIDE + (CMP_BLOCK - 1)) <= qpos) & (n_id < n_cmp)
        sc = jnp.where(valid, sc, -jnp.inf)
        m = jnp.max(sc, axis=1, keepdims=True)
        m = jnp.where(m > -jnp.inf, m, 0.0)
        p = jnp.exp(sc - m)
        pc = p / jnp.maximum(jnp.sum(p, axis=1, keepdims=True), 1e-30)
        pcb = pc.astype(BF16)
        o = [jnp.dot(pcb, vc_scr[g], preferred_element_type=F32) for g in range(NSA_KV_HEADS)]
        ocmp_ref[0] = jnp.where(row_g == 0, o[0], o[1])
        psum = jnp.concatenate(
            [jnp.sum(pc[g * NSA_GROUP:(g + 1) * NSA_GROUP], axis=0, keepdims=True)
             for g in range(NSA_KV_HEADS)]
            + [jnp.zeros((HEAD_ROWS - NSA_KV_HEADS, n_sub), F32)], axis=0)
        hi = psum.astype(BF16)
        r1 = psum - hi.astype(F32)
        mid = r1.astype(BF16)
        lo = (r1 - mid.astype(F32)).astype(BF16)
        imp_ref[0] = jnp.dot(jnp.concatenate([hi, mid, lo], axis=1), pool_ref[...],
                             preferred_element_type=F32)


def decode_cmp(cache_view, page_table, q16, pe2, w1, w2, pool_nat, *, n_cmp, qpos):
    bs, n_pages = page_table.shape
    page = cache_view.shape[1]
    past = n_pages * page
    n_sub = past // CMP_STRIDE
    nb = pool_nat.shape[1]
    half = CMP_STRIDE * HEAD_DIM

    def whole(a):
        return pl.BlockSpec(a.shape, lambda i, j, pt: (0,) * a.ndim)

    grid_spec = pltpu.PrefetchScalarGridSpec(
        num_scalar_prefetch=1,
        grid=(bs, 3),
        in_specs=[pl.BlockSpec(memory_space=pl.ANY),
                  pl.BlockSpec((1, HEAD_ROWS, HEAD_DIM), lambda i, j, pt: (i, 0, 0)),
                  whole(pe2), whole(w1), whole(w2), whole(pool_nat)],
        out_specs=(pl.BlockSpec((1, HEAD_ROWS, HEAD_DIM), lambda i, j, pt: (i, 0, 0)),
                   pl.BlockSpec((1, HEAD_ROWS, nb), lambda i, j, pt: (i, 0, 0))),
        scratch_shapes=[pltpu.VMEM((NSA_KV_HEADS, past, HEAD_DIM), F32),
                        pltpu.VMEM((NSA_KV_HEADS, past, HEAD_DIM), F32),
                        pltpu.VMEM((NSA_KV_HEADS * n_sub, half), BF16),
                        pltpu.VMEM((NSA_KV_HEADS * n_sub, half), BF16),
                        pltpu.VMEM((NSA_KV_HEADS, n_sub, HEAD_DIM), BF16),
                        pltpu.VMEM((NSA_KV_HEADS, n_sub, HEAD_DIM), BF16),
                        pltpu.SemaphoreType.DMA((2,))],
    )
    return pl.pallas_call(
        functools.partial(_decode_cmp_kernel, n_pages=n_pages, page=page, n_cmp=n_cmp, qpos=qpos),
        out_shape=(jax.ShapeDtypeStruct((bs, HEAD_ROWS, HEAD_DIM), F32),
                   jax.ShapeDtypeStruct((bs, HEAD_ROWS, nb), F32)),
        grid_spec=grid_spec,
        compiler_params=_cparams(("arbitrary", "arbitrary")),
        name="decode_cmp",
    )(page_table, cache_view, q16, pe2, w1, w2, pool_nat)


def _top_indices(score, blk, n_iter):
    rows = []
    s = score
    blk = blk.astype(F32)
    n_blk = float(score.shape[0])
    for _ in range(n_iter):
        mx = jnp.max(s, axis=0, keepdims=True)
        idx = jnp.min(jnp.where(s == mx, blk, n_blk), axis=0, keepdims=True)
        rows.append(jnp.where(mx > -jnp.inf, idx, -1.0))
        s = jnp.where(blk == idx, -jnp.inf, s)
    return rows


def _decode_select_kernel(imp_ref, km_ref, q_ref, nidx_ref, midx_ref, *, qpos):
    nbk = imp_ref.shape[0]
    blk = lax.broadcasted_iota(jnp.int32, (nbk, 1), 0)
    own = qpos // SEL_BLOCK
    open_blk = blk * SEL_BLOCK <= qpos
    forced = open_blk & ((blk == 0) | (blk == own) | (blk == own - 1))
    score = jnp.where(forced, jnp.inf, jnp.where(open_blk, imp_ref[...], -jnp.inf))
    nidx_ref[...] = jnp.concatenate(_top_indices(score, blk, N_SEL), axis=0).astype(jnp.int32)

    n_seq, n_blocks = km_ref.shape[0], km_ref.shape[1]
    n_col = n_seq * MOBA_HEADS
    lane = lax.broadcasted_iota(jnp.int32, (1, n_col), 1)

    def body(i, sct):
        q = _bf16_round(q_ref[i])
        for h in range(MOBA_HEADS):
            prod = _bf16_round(km_ref[i, :, h, :]) * q[h:h + 1]
            col = jnp.sum(prod, axis=1, keepdims=True)
            sct = jnp.where(lane == i * MOBA_HEADS + h, col, sct)
        return sct

    sct = lax.fori_loop(0, n_seq, body, jnp.zeros((n_blocks, n_col), F32))
    blk_m = lax.broadcasted_iota(jnp.int32, (n_blocks, 1), 0)
    score_m = jnp.where(blk_m < qpos // MOBA_BLOCK, sct, -jnp.inf)
    rows = _top_indices(score_m, blk_m, MOBA_TOPK)
    rows = rows + [jnp.full((1, n_col), -1.0, F32)] * (midx_ref.shape[0] - MOBA_TOPK)
    midx_ref[...] = jnp.concatenate(rows, axis=0).astype(jnp.int32)


def decode_select(imp_t, kmean, q16, *, qpos):
    n_col_a = imp_t.shape[1]
    n_col_b = kmean.shape[0] * MOBA_HEADS

    def whole(a):
        return pl.BlockSpec(a.shape, lambda i: (0,) * a.ndim)

    return pl.pallas_call(
        functools.partial(_decode_select_kernel, qpos=qpos),
        out_shape=(jax.ShapeDtypeStruct((N_SEL, n_col_a), jnp.int32),
                   jax.ShapeDtypeStruct((8, n_col_b), jnp.int32)),
        grid=(1,),
        in_specs=[whole(imp_t), whole(kmean), whole(q16)],
        out_specs=(pl.BlockSpec((N_SEL, n_col_a), lambda i: (0, 0)),
                   pl.BlockSpec((8, n_col_b), lambda i: (0, 0))),
        compiler_params=_cparams(("arbitrary",)),
        name="decode_select",
    )(imp_t, kmean, q16)


def _decode_attend_kernel(pt_ref, nidx_ref, midx_ref, nsa_ref, moba_ref,
                          qa_ref, qb_ref, ga_ref, za_ref, zb_ref, ocmp_ref, newa_ref, newb_ref, win_ref,
                          oa_ref, ob_ref, ks, vs, km, vm, sem, *, page, past, win_base):
    s = pl.program_id(0)
    n_seq = pl.num_programs(0)
    slot = lax.rem(s, 2)
    scale = HEAD_DIM ** -0.5
    blocks_per_page = page // SEL_BLOCK
    pages_per_block = MOBA_BLOCK // page
    n_cache_blk = past // SEL_BLOCK

    def nsa_copies(seq, sl, fn):
        for g in range(NSA_KV_HEADS):
            def body(i, c, g=g):
                blk = jnp.clip(nidx_ref[i, seq * NSA_KV_HEADS + g], 0, n_cache_blk - 1)
                pid = pt_ref[seq, blk // blocks_per_page]
                rows = pl.ds((blk % blocks_per_page) * SEL_BLOCK, SEL_BLOCK)
                dst = pl.ds(i * SEL_BLOCK, SEL_BLOCK)
                fn(pltpu.make_async_copy(nsa_ref.at[pid, rows, 2 * NSA_KV_HEADS + g, :],
                                         ks.at[sl, g, dst, :], sem.at[sl, 0]), 0)
                fn(pltpu.make_async_copy(nsa_ref.at[pid, rows, 3 * NSA_KV_HEADS + g, :],
                                         vs.at[sl, g, dst, :], sem.at[sl, 0]), 1)
                return c
            lax.fori_loop(0, N_SEL, body, 0)

    def moba_copies(seq, sl, fn):
        for h in range(MOBA_HEADS):
            def body(i, c, h=h):
                blk = jnp.maximum(midx_ref[i, seq * MOBA_HEADS + h], 0)
                for pp in range(pages_per_block):
                    pid = pt_ref[seq, blk * pages_per_block + pp]
                    dst = pl.ds((i * pages_per_block + pp) * page, page)
                    fn(pltpu.make_async_copy(moba_ref.at[pid, :, h, :], km.at[sl, h, dst, :], sem.at[sl, 1]), 0)
                    fn(pltpu.make_async_copy(moba_ref.at[pid, :, MOBA_HEADS + h, :],
                                             vm.at[sl, h, dst, :], sem.at[sl, 1]), 1)
                return c
            lax.fori_loop(0, MOBA_TOPK, body, 0)

    def start(cp, lane):
        cp.start(priority=lane % 2)

    def wait(cp, lane):
        cp.wait()

    @pl.when(s == 0)
    def _():
        nsa_copies(0, 0, start)
        moba_copies(0, 0, start)

    @pl.when(s + 1 < n_seq)
    def _():
        nsa_copies(s + 1, 1 - slot, start)
        moba_copies(s + 1, 1 - slot, start)

    nsa_copies(s, slot, wait)
    moba_copies(s, slot, wait)

    row = lax.broadcasted_iota(jnp.int32, (HEAD_ROWS, 1), 0)
    row_g = row >> 2
    qa = qa_ref[0].astype(BF16)
    qa_f = qa.astype(F32)
    new_a = _bf16_round(newa_ref[0])

    n_keys = N_SEL * SEL_BLOCK
    lane_blk = lax.broadcasted_iota(jnp.int32, (1, n_keys), 1) >> (SEL_BLOCK.bit_length() - 1)
    o_slc = []
    for g in range(NSA_KV_HEADS):
        k = ks[slot, g].astype(BF16)
        v = vs[slot, g].astype(BF16)
        sc = lax.dot_general(qa, k, NT_DIMS, preferred_element_type=F32) * scale
        okv = jnp.zeros((1, n_keys), jnp.int32)
        for i in range(N_SEL):
            b = nidx_ref[i, s * NSA_KV_HEADS + g]
            ok = ((b >= 0) & (b < n_cache_blk)).astype(jnp.int32)
            okv = jnp.where(lane_blk == i, ok, okv)
        s_new = jnp.sum(qa_f * new_a[g:g + 1], axis=1, keepdims=True) * scale
        pn, pn_new = _softmax_rows_with_extra(sc, okv > 0, s_new)
        o_slc.append(jnp.dot(pn.astype(BF16), v, preferred_element_type=F32)
                     + _bf16_round(pn_new) * new_a[NSA_KV_HEADS + g:NSA_KV_HEADS + g + 1])
    o_slc = jnp.where(row_g == 0, o_slc[0], o_slc[1])

    half_w = win_ref.shape[1]
    pos_e = win_base + 2 * lax.broadcasted_iota(jnp.int32, (1, half_w), 1)
    o_win = []
    for g in range(NSA_KV_HEADS):
        parts = []
        for parity in range(2):
            k = win_ref[0, :, 4 * parity + g, :].astype(BF16)
            sc = lax.dot_general(qa, k, NT_DIMS, preferred_element_type=F32) * scale
            pos = pos_e + parity
            ok = (pos <= past) & (pos > past - WINDOW) & (pos >= win_base)
            parts.append(jnp.where(ok, sc, -jnp.inf))
        s_new = jnp.sum(qa_f * new_a[4 + g:5 + g], axis=1, keepdims=True) * scale
        sc = jnp.concatenate(parts, axis=1)
        pn, pn_new = _softmax_rows_with_extra(sc, sc > -jnp.inf, s_new)
        o = _bf16_round(pn_new) * new_a[6 + g:7 + g]
        for parity in range(2):
            v = win_ref[0, :, 4 * parity + 2 + g, :].astype(BF16)
            o = o + jnp.dot(pn[:, parity * half_w:(parity + 1) * half_w].astype(BF16), v,
                            preferred_element_type=F32)
        o_win.append(o)
    o_win = jnp.where(row_g == 0, o_win[0], o_win[1])

    gates = ga_ref[0]
    o_a = gates[:, 0:1] * ocmp_ref[0] + gates[:, 1:2] * o_slc + gates[:, 2:3] * o_win
    oa_ref[0] = (o_a * za_ref[0]).astype(BF16)

    qb = qb_ref[0].astype(BF16)
    qb_f = qb.astype(F32)
    new_b = _bf16_round(newb_ref[0])
    n_keys_b = MOBA_TOPK * MOBA_BLOCK
    lane_blk_b = lax.broadcasted_iota(jnp.int32, (1, n_keys_b), 1) >> (MOBA_BLOCK.bit_length() - 1)
    o_b = jnp.zeros((HEAD_ROWS, HEAD_DIM), F32)
    for h in range(MOBA_HEADS):
        k = km[slot, h].astype(BF16)
        v = vm[slot, h].astype(BF16)
        sc = lax.dot_general(qb, k, NT_DIMS, preferred_element_type=F32) * scale
        okv = jnp.zeros((1, n_keys_b), jnp.int32)
        for i in range(MOBA_TOPK):
            ok = (midx_ref[i, s * MOBA_HEADS + h] >= 0).astype(jnp.int32)
            okv = jnp.where(lane_blk_b == i, ok, okv)
        s_new = jnp.sum(qb_f * new_b[h:h + 1], axis=1, keepdims=True) * scale
        pn, pn_new = _softmax_rows_with_extra(sc, okv > 0, s_new)
        o = (jnp.dot(pn.astype(BF16), v, preferred_element_type=F32)
             + _bf16_round(pn_new) * new_b[MOBA_HEADS + h:MOBA_HEADS + h + 1])
        o_b = jnp.where(row == h, o, o_b)
    ob_ref[0] = (o_b * zb_ref[0]).astype(BF16)


def decode_attend(nsa_view, moba_view, page_table, nidx, midx, per_seq, win_view, *, past, win_base):
    bs = page_table.shape[0]
    page = nsa_view.shape[1]
    row_spec = pl.BlockSpec((1, HEAD_ROWS, HEAD_DIM), lambda i, *_: (i, 0, 0))
    grid_spec = pltpu.PrefetchScalarGridSpec(
        num_scalar_prefetch=3,
        grid=(bs,),
        in_specs=[pl.BlockSpec(memory_space=pl.ANY), pl.BlockSpec(memory_space=pl.ANY)]
        + [row_spec] * len(per_seq)
        + [pl.BlockSpec((1,) + win_view.shape[1:], lambda i, *_: (i, 0, 0, 0))],
        out_specs=(row_spec, row_spec),
        scratch_shapes=[pltpu.VMEM((2, NSA_KV_HEADS, N_SEL * SEL_BLOCK, HEAD_DIM), F32),
                        pltpu.VMEM((2, NSA_KV_HEADS, N_SEL * SEL_BLOCK, HEAD_DIM), F32),
                        pltpu.VMEM((2, MOBA_HEADS, MOBA_TOPK * MOBA_BLOCK, HEAD_DIM), F32),
                        pltpu.VMEM((2, MOBA_HEADS, MOBA_TOPK * MOBA_BLOCK, HEAD_DIM), F32),
                        pltpu.SemaphoreType.DMA((2, 2))],
    )
    return pl.pallas_call(
        functools.partial(_decode_attend_kernel, page=page, past=past, win_base=win_base),
        out_shape=(jax.ShapeDtypeStruct((bs, HEAD_ROWS, HEAD_DIM), BF16),
                   jax.ShapeDtypeStruct((bs, HEAD_ROWS, HEAD_DIM), BF16)),
        grid_spec=grid_spec,
        compiler_params=_cparams(("arbitrary",)),
        name="decode_attend",
    )(page_table, nidx, midx, nsa_view, moba_view, *per_seq, win_view)


def _rope_tables(pos):
    half = HEAD_DIM // 2
    inv = ROPE_THETA ** (-(jnp.arange(half, dtype=F32) / half))
    ang = pos.astype(F32)[:, None] * inv[None, :]
    cos, sin = jnp.cos(ang), jnp.sin(ang)
    return jnp.concatenate([cos, cos], axis=1), jnp.concatenate([-sin, sin], axis=1)


def _pool_matrix(nc_pad, n_cmp, n_blk=LANES):
    ratio = SEL_BLOCK // CMP_STRIDE
    n_sub = CMP_BLOCK // CMP_STRIDE
    j = np.arange(n_blk)[:, None]
    n = np.arange(nc_pad)[None, :]
    pool = ((n >= ratio * j - (n_sub - 1)) & (n <= ratio * j + ratio - 1) & (n < n_cmp)).astype(np.float32)
    return jnp.asarray(np.concatenate([pool, pool, pool], axis=1), dtype=BF16)


def _pad_rows(a, rows):
    return jnp.pad(a, ((0, 0), (0, rows - a.shape[1]), (0, 0)))


def kernel(x_prompt, x_sample, cache_nsa, cache_moba, state_nsa_win, page_table, c_prompt, c_sample,
           w_ada, b_ada, norm_g, w_in, pe_k, w_phi_k1, w_phi_k2, pe_v, w_phi_v1, w_phi_v2,
           w_out_a, w_out_b, w_out, final_g):
    depth = w_in.shape[0]
    assert depth == 1, "single-layer step"
    d_model = x_prompt.shape[-1]
    bp, t_p, _ = x_prompt.shape
    bs, t_s, _ = x_sample.shape
    assert bp == 1 and t_s == 1
    page = cache_nsa.shape[2]
    n_pages = page_table.shape[1]
    past = n_pages * page
    wb = state_nsa_win.shape[2]
    w_a = NSA_HEADS * HEAD_DIM
    w_b = MOBA_HEADS * HEAD_DIM
    kv_a = NSA_KV_HEADS * HEAD_DIM
    s0 = w_a + 6 * kv_a
    s1 = s0 + 3 * NSA_HEADS
    s2 = s1 + w_a
    s3 = s2 + 3 * w_b
    s4 = s3 + w_b

    wi = w_in[0]
    w_ga = wi[:, s0:s1].reshape(d_model, NSA_KV_HEADS, 3 * NSA_GROUP)
    w_ga = jnp.pad(w_ga, ((0, 0), (0, 0), (0, LANES - 3 * NSA_GROUP))).reshape(d_model, -1)
    w_all = jnp.concatenate([wi[:, :s0], wi[:, s2:s3], wi[:, s1:s2], wi[:, s3:s4], wi[:, s4:], w_ga],
                            axis=1).astype(BF16)
    c_nsa = slice(0, s0)
    c_moba = slice(c_nsa.stop, c_nsa.stop + 3 * w_b)
    c_z = slice(c_moba.stop, c_moba.stop + w_a + w_b)
    c_gm = slice(c_z.stop, c_z.stop + 2 * d_model)
    c_ga = slice(c_gm.stop, c_gm.stop + NSA_KV_HEADS * LANES)
    w1 = jnp.stack([w_phi_k1[0], w_phi_v1[0]]).astype(BF16)
    w2 = jnp.stack([w_phi_k2[0], w_phi_v2[0]]).astype(BF16)
    half_w = CMP_STRIDE * HEAD_DIM
    pe2 = jnp.stack([pe_k[0].reshape(2, half_w), pe_v[0].reshape(2, half_w)])
    woa = w_out_a[0].astype(BF16)
    wob = w_out_b[0].astype(BF16)
    wo = w_out[0].astype(BF16)
    fg = final_g.reshape(1, d_model)
    ng = norm_g[0].reshape(1, d_model)

    c_all = jnp.concatenate([c_prompt, c_sample], axis=0)
    n_c = c_all.shape[0]
    c_all = jnp.pad(c_all, ((0, -n_c % 8), (0, 0)))
    mod = ada_mod(c_all, w_ada[0], b_ada[0].reshape(1, -1))
    shift, scale, gate = mod[:, :d_model], mod[:, d_model:2 * d_model], mod[:, 2 * d_model:]

    def project(x2d, rows, pos, tm):
        sl = slice(rows.start, rows.stop)
        h = norm_modulate(x2d, ng, scale[sl], shift[sl], tm)
        cos2, sin2 = _rope_tables(pos)
        nsa_f, nsa_b = proj_rope(h, w_all, c_nsa, cos2, sin2, tm, 512, w_a // 512, 2)
        moba_f, moba_b = proj_rope(h, w_all, c_moba, cos2, sin2, tm, 512, 2 * w_b // 512, 0)
        z = proj_act(h, w_all, c_z, "silu", tm, 512)
        g_m = proj_act(h, w_all, c_gm, "sigmoid", tm, 512)
        g_a = proj_act(h, w_all, c_ga, "sigmoid", tm, NSA_KV_HEADS * LANES)
        return nsa_f, nsa_b, moba_f, moba_b, z, g_m, g_a

    def finish(oz_a, oz_b, g_m, x2d, rows, tm):
        merged = merge_branches(oz_a, oz_b, g_m, woa, wob, tm)
        return final_out(merged, x2d, gate[rows.start:rows.stop], wo, fg, tm)

    assert t_p % MOBA_BLOCK == 0
    xp = x_prompt.reshape(t_p, d_model)
    tm_p = 1024 if t_p % 1024 == 0 else 256
    nsa_f, nsa_b, moba_f, moba_b, z_p, gm_p, ga_p = project(
        xp, slice(0, 1), jnp.arange(t_p, dtype=jnp.int32), tm_p)
    n_cmp_p = (t_p - CMP_BLOCK) // CMP_STRIDE + 1
    n_sub_p = t_p // CMP_STRIDE
    assert n_cmp_p == n_sub_p - 1 and n_sub_p % 8 == 0
    sub_p = nsa_f[:, w_a:w_a + 2 * kv_a].reshape(n_sub_p, CMP_STRIDE, 4, HEAD_DIM)
    sub_p = sub_p.transpose(2, 0, 1, 3).reshape(1, 4, n_sub_p, half_w)
    kc_p = compress(sub_p, pe2, w1, w2)
    pool_p = _pool_matrix(n_sub_p, n_cmp_p)
    tk_p = max(t for t in (1024, 512, 256) if t_p % t == 0)
    cq_p = min(256, tk_p)
    assert (t_p - 1) // SEL_BLOCK < LANES
    win_p = jnp.pad(nsa_b[:, w_a + 4 * kv_a:], ((WINDOW, 0), (0, 0)))[None]
    nsa_b3 = nsa_b[None]
    oz_a = nsa_attention(
        nsa_b3, 0, ga_p[None], z_p[None], 0, kc_p,
        nsa_b3, (w_a + 2 * kv_a) // LANES, (w_a + 3 * kv_a) // LANES,
        win_p, 0, NSA_KV_HEADS, pool_p,
        cq=cq_p, tk=tk_p, qpos0=0, n_cmp=n_cmp_p, win_base=0)
    moba_f3 = moba_f[None]
    moba_b3 = moba_b[None]
    km_p = block_means(moba_f3, 1, w_b, blocks_per_step=8 if (t_p // MOBA_BLOCK) % 8 == 0 else t_p // MOBA_BLOCK)
    km_p = _pad_rows(km_p, LANES).astype(BF16)
    cqm_p = tk_p
    n_phys = cache_nsa.shape[1]
    moba_view = cache_moba.reshape(n_phys, page, 2 * MOBA_HEADS, HEAD_DIM)
    oz_b, kmean_s = moba_attention(moba_b3, 0, z_p[None], w_a // LANES, km_p, moba_b3, w_b // LANES,
                                   2 * w_b // LANES, page_table, moba_view, cq=cqm_p, tk=tk_p, qpos0=0)
    y_prompt = finish(oz_a[0], oz_b[0], gm_p, xp, slice(0, 1), 256).reshape(1, t_p, d_model)
    new_nsa_prompt = nsa_f[:, w_a:w_a + 4 * kv_a].reshape(1, 1, t_p, 4, NSA_KV_HEADS, HEAD_DIM)
    new_moba_prompt = moba_f[:, w_b:].reshape(1, 1, t_p, 2, MOBA_HEADS, HEAD_DIM)
    wlen = min(WINDOW, t_p)
    new_win_prompt = nsa_f[t_p - wlen:, w_a + 4 * kv_a:].reshape(1, 1, wlen, 2, NSA_KV_HEADS, HEAD_DIM)

    xs = x_sample.reshape(bs, d_model)
    pos_s = jnp.full((bs,), past, jnp.int32)
    nsa_fs, _, moba_fs, _, z_s, gm_s, ga_s = project(xs, slice(1, 1 + bs), pos_s, bs)
    l_s = past + 1
    n_cmp_s = (l_s - CMP_BLOCK) // CMP_STRIDE + 1
    n_sub_s = past // CMP_STRIDE
    assert n_cmp_s == n_sub_s - 1 and n_sub_s % 8 == 0
    assert past % MOBA_BLOCK == 0 and MOBA_BLOCK % page == 0 and page % SEL_BLOCK == 0
    assert wb == WINDOW and wb % 2 == 0 and past >= WINDOW
    nsa_view = cache_nsa.reshape(n_phys, page, 4 * NSA_KV_HEADS, HEAD_DIM)
    win_view = state_nsa_win.reshape(bs, wb // 2, 4 * NSA_KV_HEADS, HEAD_DIM)

    def head_rows(a2d):
        a = a2d.reshape(bs, -1, HEAD_DIM)
        return jnp.pad(a, ((0, 0), (0, HEAD_ROWS - a.shape[1]), (0, 0)))

    qa16 = head_rows(nsa_fs[:, :w_a])
    qb16 = head_rows(moba_fs[:, :w_b])
    za16 = head_rows(z_s[:, :w_a])
    zb16 = head_rows(z_s[:, w_a:])
    ga16 = ga_s.reshape(bs, NSA_KV_HEADS, LANES)[:, :, :3 * NSA_GROUP].reshape(bs, NSA_HEADS, 3)
    ga16 = jnp.pad(ga16, ((0, 0), (0, HEAD_ROWS - NSA_HEADS), (0, LANES - 3)))
    new_a16 = head_rows(nsa_fs[:, w_a + 2 * kv_a:])
    new_b16 = head_rows(moba_fs[:, w_b:])

    n_sel_blk = -(-(past // SEL_BLOCK + 1) // LANES) * LANES
    pool_s = _pool_matrix(n_sub_s, n_cmp_s, n_sel_blk).T
    ocmp16, imp = decode_cmp(nsa_view, page_table, qa16, pe2, w1, w2, pool_s, n_cmp=n_cmp_s, qpos=past)
    imp_t = imp[:, :NSA_KV_HEADS, :].transpose(2, 0, 1).reshape(n_sel_blk, bs * NSA_KV_HEADS)
    nidx, midx = decode_select(imp_t, kmean_s, qb16, qpos=past)
    oa16, ob16 = decode_attend(nsa_view, moba_view, page_table, nidx, midx,
                               [qa16, qb16, ga16, za16, zb16, ocmp16, new_a16, new_b16], win_view,
                               past=past, win_base=past - wb)
    oz_as = oa16[:, :NSA_HEADS].reshape(bs, w_a)
    oz_bs = ob16[:, :MOBA_HEADS].reshape(bs, w_b)
    y_sample = finish(oz_as, oz_bs, gm_s, xs, slice(1, 1 + bs), bs).reshape(bs, 1, d_model)
    new_nsa_sample = nsa_fs[:, w_a:w_a + 4 * kv_a].reshape(1, bs, 1, 4, NSA_KV_HEADS, HEAD_DIM)
    new_moba_sample = moba_fs[:, w_b:].reshape(1, bs, 1, 2, MOBA_HEADS, HEAD_DIM)
    new_win_sample = jnp.concatenate(
        [state_nsa_win[:, :, 1:],
         nsa_fs[:, w_a + 4 * kv_a:].reshape(1, bs, 1, 2, NSA_KV_HEADS, HEAD_DIM)], axis=2)

    return (y_prompt, y_sample, new_nsa_prompt, new_nsa_sample, new_moba_prompt, new_moba_sample,
            new_win_prompt, new_win_sample)
```
